```python
import math
import jax, jax.numpy as jnp
from jax import lax
import numpy as np

D_MODEL = 4096
BATCH = 4
SEQ = 2048
DEPTH = 1
DEC_BATCH = 128
DEC_SEQ = 4
PAST_LEN = 16384
PAGE_SIZE = 128

SSM_EXPAND = 2
D_INNER = SSM_EXPAND * D_MODEL
SSM_HEAD_DIM = 64
SSM_HEADS = D_INNER // SSM_HEAD_DIM
SSM_GROUPS = 8
D_STATE = 128
SSM_CONV = 4
SSD_CHUNK = 128
SSD_CONV_DIM = D_INNER + 2 * SSM_GROUPS * D_STATE
D_CONF = D_MODEL
CONF_KERNEL = 31
PEER_HEADS = 8
N_KEYS = 128
N_EXPERTS = N_KEYS * N_KEYS
PEER_TOPK = 16
PEER_QDIM = 256
PEER_HALF = PEER_QDIM // 2
PEER_BLOCK = 128
N_MOD = 6
SPLIT_Z = D_INNER
SPLIT_XBC = SPLIT_Z + SSD_CONV_DIM
SPLIT_DT = SPLIT_XBC + SSM_HEADS
SPLIT_CONF = SPLIT_DT + 2 * D_CONF
IN_COLS = SPLIT_CONF + 2 * D_MODEL
EPS = 1e-6

kernel_name = 'hybrid_ssd_conformer_peer_adaln_step'


def rms_norm(x, g):
    x32 = x.astype(jnp.float32)
    y = x32 * lax.rsqrt(jnp.mean(x32 * x32, axis=-1, keepdims=True) + EPS)
    return (y * g.astype(jnp.float32)).astype(x.dtype)


def layer_norm(x, g, b):
    x32 = x.astype(jnp.float32)
    xc = x32 - jnp.mean(x32, axis=-1, keepdims=True)
    var = jnp.mean(xc * xc, axis=-1, keepdims=True)
    return (xc * lax.rsqrt(var + EPS) * g.astype(jnp.float32) + b.astype(jnp.float32)).astype(x.dtype)


def causal_dwconv(buf, x, w, b):
    xp = jnp.concatenate([buf.astype(x.dtype), x], axis=1)
    y = lax.conv_general_dilated(xp, w.astype(x.dtype)[:, None, :], (1,), 'VALID',
                                 dimension_numbers=('NWC', 'WIO', 'NWC'),
                                 feature_group_count=x.shape[-1])
    return y + b.astype(x.dtype), xp[:, xp.shape[1] - (w.shape[0] - 1):]


def ssd_scan(x, dt, a, bm, cm, s0):
    bsz, l, h, p = x.shape
    g, n = bm.shape[2], bm.shape[3]
    r = h // g
    q = SSD_CHUNK if l % SSD_CHUNK == 0 else l
    nc = l // q
    f32 = jnp.float32
    xc = x.astype(f32).reshape(bsz, nc, q, g, r, p)
    dtc = dt.astype(f32).reshape(bsz, nc, q, g, r)
    bc = bm.astype(f32).reshape(bsz, nc, q, g, n)
    cc = cm.astype(f32).reshape(bsz, nc, q, g, n)
    acum = jnp.cumsum(dtc * a.astype(f32).reshape(g, r), axis=2)
    causal = jnp.tril(jnp.ones((q, q), dtype=bool))[:, :, None, None]
    seg = acum[:, :, :, None] - acum[:, :, None]
    decay = jnp.exp(jnp.where(causal, seg, -jnp.inf))
    scores = jnp.einsum('bcign,bcjgn->bcijg', cc, bc)
    wgt = scores[..., None] * decay * dtc[:, :, None]
    y_diag = jnp.einsum('bcijgr,bcjgrp->bcigrp', wgt, xc)
    decay_end = jnp.exp(acum[:, :, -1:] - acum)
    chunk_s = jnp.einsum('bcjgn,bcjgr,bcjgrp->bcgrpn', bc, dtc * decay_end, xc)
    chunk_decay = jnp.exp(acum[:, :, -1])

    def step(s, inp):
        cs, cd = inp
        return cd[..., None, None] * s + cs, s

    s_init = s0.astype(f32).reshape(bsz, g, r, p, n)
    s_fin, s_prev = lax.scan(step, s_init, (jnp.moveaxis(chunk_s, 1, 0), jnp.moveaxis(chunk_decay, 1, 0)))
    s_prev = jnp.moveaxis(s_prev, 0, 1)
    y_off = jnp.einsum('bcign,bcgrpn,bcigr->bcigrp', cc, s_prev, jnp.exp(acum))
    y = (y_diag + y_off).reshape(bsz, l, h, p)
    return y, s_fin.reshape(bsz, h, p, n)


def peer(h, wq, keys1, keys2, u, v):
    bsz, l, d = h.shape
    n = bsz * l
    npad = -(-n // PEER_BLOCK) * PEER_BLOCK
    hf = jnp.pad(h.reshape(n, d), ((0, npad - n), (0, 0)))
    qry = (hf @ wq).reshape(npad, PEER_HEADS, 2, PEER_HALF).astype(jnp.float32)
    s1 = jnp.einsum('thd,hkd->thk', qry[:, :, 0], keys1.astype(jnp.float32))
    s2 = jnp.einsum('thd,hkd->thk', qry[:, :, 1], keys2.astype(jnp.float32))
    v1, i1 = lax.top_k(s1, PEER_TOPK)
    v2, i2 = lax.top_k(s2, PEER_TOPK)
    cand_s = (v1[..., :, None] + v2[..., None, :]).reshape(npad, PEER_HEADS, PEER_TOPK * PEER_TOPK)
    cand_i = (i1[..., :, None] * N_KEYS + i2[..., None, :]).reshape(npad, PEER_HEADS, PEER_TOPK * PEER_TOPK)
    top_s, pos = lax.top_k(cand_s, PEER_TOPK)
    experts = jnp.take_along_axis(cand_i, pos, axis=-1)
    gates = jax.nn.softmax(top_s, axis=-1).astype(h.dtype)
    nb = npad // PEER_BLOCK

    def block(args):
        hb, eb, gb = args
        ub = jnp.take(u, eb, axis=0)
        act = jax.nn.gelu(jnp.einsum('td,thkd->thk', hb, ub), approximate=False)
        vb = jnp.take(v, eb, axis=0)
        return jnp.einsum('thk,thkd->td', gb * act, vb)

    out = lax.map(block, (hf.reshape(nb, PEER_BLOCK, d),
                          experts.reshape(nb, PEER_BLOCK, PEER_HEADS, PEER_TOPK),
                          gates.reshape(nb, PEER_BLOCK, PEER_HEADS, PEER_TOPK)))
    return out.reshape(npad, d)[:n].reshape(bsz, l, d)


def hybrid_layer(x, c, ssm0, ssd_buf0, conf_buf0, p):
    bsz, l, _ = x.shape
    mod = (jax.nn.silu(c) @ p['w_ada'] + p['b_ada'])[:, None, :]
    sh1, sc1, g1, sh2, sc2, g2 = jnp.split(mod, N_MOD, axis=-1)
    h = rms_norm(x, p['norm1_g']) * (1 + sc1) + sh1
    proj = h @ p['w_in']
    z, xbc, dt_raw, conf_in, gates = jnp.split(proj, [SPLIT_Z, SPLIT_XBC, SPLIT_DT, SPLIT_CONF], axis=-1)
    xbc, ssd_buf = causal_dwconv(ssd_buf0, xbc, p['ssd_conv_w'], p['ssd_conv_b'])
    xbc = jax.nn.silu(xbc)
    xs, bm, cm = jnp.split(xbc, [D_INNER, D_INNER + SSM_GROUPS * D_STATE], axis=-1)
    xs = xs.reshape(bsz, l, SSM_HEADS, SSM_HEAD_DIM)
    bm = bm.reshape(bsz, l, SSM_GROUPS, D_STATE)
    cm = cm.reshape(bsz, l, SSM_GROUPS, D_STATE)
    dt = jax.nn.softplus(dt_raw.astype(jnp.float32) + p['dt_bias'].astype(jnp.float32))
    a = -jnp.exp(p['a_log'].astype(jnp.float32))
    y, ssm = ssd_scan(xs, dt, a, bm, cm, ssm0)
    y = y + p['d_skip'].astype(jnp.float32)[:, None] * xs.astype(jnp.float32)
    y = y.reshape(bsz, l, D_INNER).astype(x.dtype)
    y_ssd = rms_norm(y * jax.nn.silu(z), p['ssd_norm_g']) @ p['w_ssd_out']
    ca, cg = jnp.split(conf_in, 2, axis=-1)
    uc = ca * jax.nn.sigmoid(cg)
    vc, conf_buf = causal_dwconv(conf_buf0, uc, p['conf_dw_w'], p['conf_dw_b'])
    vc = jax.nn.silu(layer_norm(vc, p['conf_ln_g'], p['conf_ln_b']))
    y_conf = vc @ p['w_conf_out']
    ga, gb = jnp.split(gates, 2, axis=-1)
    mixed = (jax.nn.sigmoid(ga) * y_ssd + jax.nn.sigmoid(gb) * y_conf) @ p['w_out']
    x = x + g1 * mixed
    h2 = rms_norm(x, p['norm2_g']) * (1 + sc2) + sh2
    x = x + g2 * peer(h2, p['peer_wq'], p['peer_keys1'], p['peer_keys2'], p['peer_u'], p['peer_v'])
    return x, ssm.astype(ssm0.dtype), ssd_buf, conf_buf


def setup_inputs(seed: int = 0) -> dict:
    key = jax.random.key(seed)
    ks = list(jax.random.split(key, 40))

    def nrm(i, shape, scale):
        return jax.random.normal(ks[i], shape, jnp.float32) * scale

    dt0 = jnp.exp(jax.random.uniform(ks[30], (DEPTH, SSM_HEADS), jnp.float32)
                  * (math.log(0.1) - math.log(0.001)) + math.log(0.001))
    dt_bias = dt0 + jnp.log(-jnp.expm1(-dt0))
    a_log = jnp.log(jax.random.uniform(ks[31], (DEPTH, SSM_HEADS), jnp.float32, 1.0, 16.0))
    return {
        'x_prompt': nrm(0, (BATCH, SEQ, D_MODEL), 1.0),
        'x_sample': nrm(1, (DEC_BATCH, DEC_SEQ, D_MODEL), 1.0),
        'c_prompt': nrm(2, (BATCH, D_MODEL), 1.0),
        'c_sample': nrm(3, (DEC_BATCH, D_MODEL), 1.0),
        'state_ssm': nrm(4, (DEPTH, DEC_BATCH, SSM_HEADS, SSM_HEAD_DIM, D_STATE), 0.5),
        'state_ssd_conv': nrm(5, (DEPTH, DEC_BATCH, SSM_CONV - 1, SSD_CONV_DIM), 1.0),
        'state_conf_conv': nrm(6, (DEPTH, DEC_BATCH, CONF_KERNEL - 1, D_CONF), 0.5),
        'w_ada': nrm(7, (DEPTH, D_MODEL, N_MOD * D_MODEL), 0.5 * D_MODEL ** -0.5),
        'b_ada': nrm(8, (DEPTH, N_MOD * D_MODEL), 0.01),
        'norm1_g': 1.0 + nrm(9, (DEPTH, D_MODEL), 0.02),
        'w_in': nrm(10, (DEPTH, D_MODEL, IN_COLS), D_MODEL ** -0.5),
        'ssd_conv_w': nrm(11, (DEPTH, SSM_CONV, SSD_CONV_DIM), SSM_CONV ** -0.5),
        'ssd_conv_b': nrm(12, (DEPTH, SSD_CONV_DIM), 0.01),
        'dt_bias': dt_bias,
        'a_log': a_log,
        'd_skip': 1.0 + nrm(13, (DEPTH, SSM_HEADS), 0.1),
        'ssd_norm_g': 1.0 + nrm(14, (DEPTH, D_INNER), 0.02),
        'w_ssd_out': nrm(15, (DEPTH, D_INNER, D_MODEL), D_INNER ** -0.5),
        'conf_dw_w': nrm(16, (DEPTH, CONF_KERNEL, D_CONF), CONF_KERNEL ** -0.5),
        'conf_dw_b': nrm(17, (DEPTH, D_CONF), 0.01),
        'conf_ln_g': 1.0 + nrm(18, (DEPTH, D_CONF), 0.02),
        'conf_ln_b': nrm(19, (DEPTH, D_CONF), 0.01),
        'w_conf_out': nrm(20, (DEPTH, D_CONF, D_MODEL), D_CONF ** -0.5),
        'w_out': nrm(21, (DEPTH, D_MODEL, D_MODEL), D_MODEL ** -0.5),
        'norm2_g': 1.0 + nrm(22, (DEPTH, D_MODEL), 0.02),
        'peer_wq': nrm(23, (DEPTH, D_MODEL, PEER_HEADS * PEER_QDIM), D_MODEL ** -0.5),
        'peer_keys1': nrm(24, (DEPTH, PEER_HEADS, N_KEYS, PEER_HALF), PEER_HALF ** -0.5),
        'peer_keys2': nrm(25, (DEPTH, PEER_HEADS, N_KEYS, PEER_HALF), PEER_HALF ** -0.5),
        'peer_u': nrm(26, (DEPTH, N_EXPERTS, D_MODEL), D_MODEL ** -0.5),
        'peer_v': nrm(27, (DEPTH, N_EXPERTS, D_MODEL), PEER_HEADS ** -0.5),
        'final_norm_g': 1.0 + nrm(28, (D_MODEL,), 0.02),
    }


def reference(x_prompt, x_sample, c_prompt, c_sample, state_ssm, state_ssd_conv, state_conf_conv,
              w_ada, b_ada, norm1_g, w_in, ssd_conv_w, ssd_conv_b, dt_bias, a_log, d_skip,
              ssd_norm_g, w_ssd_out, conf_dw_w, conf_dw_b, conf_ln_g, conf_ln_b, w_conf_out,
              w_out, norm2_g, peer_wq, peer_keys1, peer_keys2, peer_u, peer_v, final_norm_g):
    bp = x_prompt.shape[0]
    xp, xs = x_prompt, x_sample
    ssm_p, sconv_p, cconv_p = [], [], []
    ssm_s, sconv_s, cconv_s = [], [], []
    for i in range(DEPTH):
        p = {'w_ada': w_ada[i], 'b_ada': b_ada[i], 'norm1_g': norm1_g[i], 'w_in': w_in[i],
             'ssd_conv_w': ssd_conv_w[i], 'ssd_conv_b': ssd_conv_b[i], 'dt_bias': dt_bias[i],
             'a_log': a_log[i], 'd_skip': d_skip[i], 'ssd_norm_g': ssd_norm_g[i],
             'w_ssd_out': w_ssd_out[i], 'conf_dw_w': conf_dw_w[i], 'conf_dw_b': conf_dw_b[i],
             'conf_ln_g': conf_ln_g[i], 'conf_ln_b': conf_ln_b[i], 'w_conf_out': w_conf_out[i],
             'w_out': w_out[i], 'norm2_g': norm2_g[i], 'peer_wq': peer_wq[i],
             'peer_keys1': peer_keys1[i], 'peer_keys2': peer_keys2[i],
             'peer_u': peer_u[i], 'peer_v': peer_v[i]}
        z_ssm = jnp.zeros((bp, SSM_HEADS, SSM_HEAD_DIM, D_STATE), x_prompt.dtype)
        z_sconv = jnp.zeros((bp, SSM_CONV - 1, SSD_CONV_DIM), x_prompt.dtype)
        z_cconv = jnp.zeros((bp, CONF_KERNEL - 1, D_CONF), x_prompt.dtype)
        xp, s1, s2, s3 = hybrid_layer(xp, c_prompt, z_ssm, z_sconv, z_cconv, p)
        ssm_p.append(s1)
        sconv_p.append(s2)
        cconv_p.append(s3)
        xs, t1, t2, t3 = hybrid_layer(xs, c_sample, state_ssm[i], state_ssd_conv[i], state_conf_conv[i], p)
        ssm_s.append(t1)
        sconv_s.append(t2)
        cconv_s.append(t3)
    y_prompt = rms_norm(xp, final_norm_g)
    y_sample = rms_norm(xs, final_norm_g)
    return (y_prompt, y_sample,
            jnp.stack(ssm_p, 0), jnp.stack(sconv_p, 0), jnp.stack(cconv_p, 0),
            jnp.stack(ssm_s, 0), jnp.stack(sconv_s, 0), jnp.stack(cconv_s, 0))
```

```python
import functools

import jax
import jax.numpy as jnp
from jax import lax
from jax.experimental import pallas as pl
from jax.experimental.pallas import tpu as pltpu

F32 = jnp.float32
BF16 = jnp.bfloat16
EPS = 1e-6
PEER_TOPK = 16
SSD_CHUNK = 128
NEG = -3.0e38
V7X_VMEM_LIMIT_BYTES = 56 * 1024 * 1024
SUBLANES = 8

NT_DIMS = (((1,), (1,)), ((), ()))
TN_DIMS = (((0,), (0,)), ((), ()))


def _cp(*sem):
    return pltpu.CompilerParams(dimension_semantics=sem,
                                vmem_limit_bytes=V7X_VMEM_LIMIT_BYTES)


def _tile(n, pref, mult=128):
    best = None
    t = mult
    while t <= min(n, pref):
        if n % t == 0:
            best = t
        t += mult
    return best if best is not None else n


def _silu(x):
    return x * jax.nn.sigmoid(x)


def _split3(a):
    hi = a.astype(BF16)
    r = a - hi.astype(F32)
    mid = r.astype(BF16)
    lo = (r - mid.astype(F32)).astype(BF16)
    return hi, mid, lo


def _dot01_left(m01, a):
    out = None
    for p in _split3(a):
        t = jnp.dot(m01, p, preferred_element_type=F32)
        out = t if out is None else out + t
    return out


def _dot01_right(a, m01):
    out = None
    for p in _split3(a):
        t = jnp.dot(p, m01, preferred_element_type=F32)
        out = t if out is None else out + t
    return out


class _Group:
    def __init__(self, n_tokens, seq_len, per_token, mods):
        self.T = n_tokens
        self.L = seq_len
        self.per_token = per_token
        self.mods = mods

    def mod_spec(self, tm, tn):
        if self.per_token:
            return pl.BlockSpec((1, tm, tn), lambda i, j: (0, i, j))
        tiles_per_batch = self.L // tm
        return pl.BlockSpec((1, 1, tn), lambda i, j: (i // tiles_per_batch, 0, j))

    def row_tile(self, pref):
        return _tile(self.T if self.per_token else self.L, pref, SUBLANES)


def _ada_kernel(c_ref, w_ref, b_ref, o_ref):
    a = _silu(c_ref[...]).astype(BF16)
    o_ref[...] = jnp.dot(a, w_ref[...].astype(BF16), preferred_element_type=F32) + b_ref[...]


def _adaln(c, w, b):
    m, d = c.shape
    n = w.shape[1]
    tn = _tile(n, 1024)
    return pl.pallas_call(
        _ada_kernel,
        grid=(n // tn,),
        in_specs=[pl.BlockSpec((m, d), lambda j: (0, 0)),
                  pl.BlockSpec((d, tn), lambda j: (0, j)),
                  pl.BlockSpec((1, tn), lambda j: (0, j))],
        out_specs=pl.BlockSpec((m, tn), lambda j: (0, j)),
        out_shape=jax.ShapeDtypeStruct((m, n), F32),
        compiler_params=_cp("arbitrary"),
        name="adaln",
    )(c, w, b.reshape(1, n))


def _modnorm_kernel(x_ref, g_ref, sc_ref, sh_ref, o_ref):
    x = x_ref[...]
    y = x * lax.rsqrt(jnp.mean(x * x, axis=-1, keepdims=True) + EPS) * g_ref[...]
    o_ref[...] = (y * (1.0 + sc_ref[0]) + sh_ref[0]).astype(o_ref.dtype)


def _modnorm(x, g, grp, k_scale, k_shift):
    t, d = x.shape
    tm = grp.row_tile(512)
    return pl.pallas_call(
        _modnorm_kernel,
        grid=(t // tm, 1),
        in_specs=[pl.BlockSpec((tm, d), lambda i, j: (i, 0)),
                  pl.BlockSpec((1, d), lambda i, j: (0, 0)),
                  grp.mod_spec(tm, d), grp.mod_spec(tm, d)],
        out_specs=pl.BlockSpec((tm, d), lambda i, j: (i, 0)),
        out_shape=jax.ShapeDtypeStruct((t, d), BF16),
        compiler_params=_cp("arbitrary", "arbitrary"),
        name="modnorm",
    )(x, g.reshape(1, d), grp.mods[k_scale], grp.mods[k_shift])


def _mm_kernel(a_ref, w_ref, *rest, epi, n_extra):
    extras = rest[:n_extra]
    o_ref = rest[n_extra]
    acc = jnp.dot(a_ref[...], w_ref[...], preferred_element_type=F32)
    vals = [e[0] if len(e.shape) == 3 else e[...] for e in extras]
    o_ref[...] = epi(acc, *vals).astype(o_ref.dtype)


def _mm(a, w, *, tm, tn, out_dtype, epi=None, extras=(), name="mm"):
    t, k = a.shape
    n = w.shape[1]
    if epi is None:
        epi = lambda acc: acc
    return pl.pallas_call(
        functools.partial(_mm_kernel, epi=epi, n_extra=len(extras)),
        grid=(t // tm, n // tn),
        in_specs=[pl.BlockSpec((tm, k), lambda i, j: (i, 0)),
                  pl.BlockSpec((k, tn), lambda i, j: (0, j))] + [s for _, s in extras],
        out_specs=pl.BlockSpec((tm, tn), lambda i, j: (i, j)),
        out_shape=jax.ShapeDtypeStruct((t, n), out_dtype),
        compiler_params=_cp("arbitrary", "arbitrary"),
        name=name,
    )(a, w, *[x for x, _ in extras])


def _tile_spec(tm, tn, col_off):
    assert col_off % tn == 0
    off = col_off // tn
    return pl.BlockSpec((tm, tn), lambda i, j: (i, j + off))


def _row_spec(tn):
    return pl.BlockSpec((1, tn), lambda i, j: (0, j))


def _ssdconv_prompt_kernel(x_ref, hb_ref, w_ref, b_ref, o_ref, u_scr, *, ks):
    seq = x_ref.shape[1]
    u_scr[0:SUBLANES, :] = hb_ref[0]
    u_scr[SUBLANES:, :] = x_ref[0].astype(F32)
    acc = b_ref[...]
    for k in range(ks):
        off = SUBLANES - (ks - 1) + k
        acc = acc + w_ref[k:k + 1, :] * u_scr[off:off + seq, :]
    o_ref[0] = _silu(acc).astype(o_ref.dtype)


def _ssdconv_prompt(proj3, hb, w, b, col_off, out_dtype):
    nb, seq, _ = proj3.shape
    ks, c = w.shape
    tc = _tile(c, 256)
    assert col_off % tc == 0
    off = col_off // tc
    return pl.pallas_call(
        functools.partial(_ssdconv_prompt_kernel, ks=ks),
        grid=(nb, c // tc),
        in_specs=[pl.BlockSpec((1, seq, tc), lambda bi, ci: (bi, 0, ci + off)),
                  pl.BlockSpec((1, SUBLANES, tc), lambda bi, ci: (bi, 0, ci)),
                  pl.BlockSpec((ks, tc), lambda bi, ci: (0, ci)),
                  pl.BlockSpec((1, tc), lambda bi, ci: (0, ci))],
        out_specs=pl.BlockSpec((1, seq, tc), lambda bi, ci: (bi, 0, ci)),
        out_shape=jax.ShapeDtypeStruct((nb, seq, c), out_dtype),
        scratch_shapes=[pltpu.VMEM((seq + SUBLANES, tc), F32)],
        compiler_params=_cp("arbitrary", "arbitrary"),
        name="ssdconv_prompt",
    )(proj3, hb, w, b.reshape(1, c))


def _ssdconv_sample_kernel(x_ref, buf_ref, w_ref, b_ref, o_ref, *, ks):
    steps = x_ref.shape[0]
    slabs = [buf_ref[m] for m in range(ks - 1)] + [x_ref[t].astype(F32) for t in range(steps)]
    for t in range(steps):
        acc = b_ref[...]
        for k in range(ks):
            acc = acc + w_ref[k:k + 1, :] * slabs[t + k]
        o_ref[t] = _silu(acc).astype(o_ref.dtype)


def _ssdconv_sample(proj3, buf_t, w, b, col_off, out_dtype):
    steps, nb, _ = proj3.shape
    ks, c = w.shape
    tc = _tile(c, 512)
    assert col_off % tc == 0
    off = col_off // tc
    return pl.pallas_call(
        functools.partial(_ssdconv_sample_kernel, ks=ks),
        grid=(c // tc,),
        in_specs=[pl.BlockSpec((steps, nb, tc), lambda ci: (0, 0, ci + off)),
                  pl.BlockSpec((ks - 1, nb, tc), lambda ci: (0, 0, ci)),
                  pl.BlockSpec((ks, tc), lambda ci: (0, ci)),
                  pl.BlockSpec((1, tc), lambda ci: (0, ci))],
        out_specs=pl.BlockSpec((steps, nb, tc), lambda ci: (0, 0, ci)),
        out_shape=jax.ShapeDtypeStruct((steps, nb, c), out_dtype),
        compiler_params=_cp("arbitrary"),
        name="ssdconv_sample",
    )(proj3, buf_t, w, b.reshape(1, c))


def _ssd_prompt_kernel(x_ref, b_ref, c_ref, dt_ref, dtT_ref, arow_ref, acol_ref, dsk_ref, s0_ref,
                       y_ref, sfin_ref, s_scr, *, heads, hdim):
    ci = pl.program_id(2)
    q = x_ref.shape[0]
    n_state = b_ref.shape[1]

    @pl.when(ci == 0)
    def _():
        s_scr[...] = s0_ref[0]

    dt = dt_ref[0]
    dt_t = dtT_ref[0]
    ii = lax.broadcasted_iota(jnp.int32, (q, q), 0)
    jj = lax.broadcasted_iota(jnp.int32, (q, q), 1)
    lower = jnp.where(ii >= jj, 1.0, 0.0).astype(BF16)
    upper = jnp.where(ii <= jj, 1.0, 0.0).astype(BF16)
    acum = _dot01_left(lower, dt * arow_ref[0])
    acum_t = _dot01_right(dt_t * acol_ref[0], upper)

    def rows(v):
        return jnp.concatenate(
            [jnp.broadcast_to(v[r:r + 1, :], (hdim, v.shape[1])) for r in range(heads)], axis=0)

    x_t = x_ref[...].astype(F32).T
    bm = b_ref[...]
    cm = c_ref[...]
    scores_t = lax.dot_general(bm, cm, NT_DIMS, preferred_element_type=F32)
    causal = ii <= jj
    last = acum_t[:, q - 1:q]
    xdt_t = (x_t * rows(dt_t)).astype(BF16)
    ydiag = []
    for r in range(heads):
        seg = acum_t[r:r + 1, :] - acum[:, r:r + 1]
        dec = jnp.exp(jnp.where(causal, seg, NEG))
        wgt = (scores_t * dec).astype(BF16)
        ydiag.append(jnp.dot(xdt_t[r * hdim:(r + 1) * hdim, :], wgt, preferred_element_type=F32))
    s_prev = s_scr[...]
    yoff_t = lax.dot_general(s_prev.astype(BF16), cm, NT_DIMS, preferred_element_type=F32)
    y_t = jnp.concatenate(ydiag, axis=0) + yoff_t * rows(jnp.exp(acum_t)) + dsk_ref[0] * x_t
    y_ref[...] = y_t.T.astype(y_ref.dtype)

    coef_t = dt_t * jnp.exp(last - acum_t)
    chunk_s = jnp.dot((x_t * rows(coef_t)).astype(BF16), bm, preferred_element_type=F32)
    cd = jnp.broadcast_to(jnp.exp(last), (heads, n_state))
    s_new = rows(cd) * s_prev + chunk_s
    s_scr[...] = s_new

    @pl.when(ci == pl.num_programs(2) - 1)
    def _():
        sfin_ref[0] = s_new


def _ssd_prompt(xbc, dt, a_log, d_skip, s0, nb, seq, groups, heads, hdim, n_state):
    t = xbc.shape[0]
    q = SSD_CHUNK if seq % SSD_CHUNK == 0 else seq
    nc = seq // q
    rp = heads * hdim
    d_inner = groups * rp
    a = -jnp.exp(a_log.astype(F32))
    dtg = dt.reshape(t, groups, heads).transpose(1, 0, 2)
    dtg_t = dt.reshape(t, groups, heads).transpose(1, 2, 0)
    a_row = a.reshape(groups, 1, heads)
    a_col = a.reshape(groups, heads, 1)
    dsk = jnp.broadcast_to(jnp.repeat(d_skip.astype(F32), hdim).reshape(groups, rp, 1), (groups, rp, q))
    b_off = d_inner // n_state
    c_off = (d_inner + groups * n_state) // n_state
    return pl.pallas_call(
        functools.partial(_ssd_prompt_kernel, heads=heads, hdim=hdim),
        grid=(nb, groups, nc),
        in_specs=[pl.BlockSpec((q, rp), lambda b, g, c: (b * nc + c, g)),
                  pl.BlockSpec((q, n_state), lambda b, g, c: (b * nc + c, b_off + g)),
                  pl.BlockSpec((q, n_state), lambda b, g, c: (b * nc + c, c_off + g)),
                  pl.BlockSpec((1, q, heads), lambda b, g, c: (g, b * nc + c, 0)),
                  pl.BlockSpec((1, heads, q), lambda b, g, c: (g, 0, b * nc + c)),
                  pl.BlockSpec((1, 1, heads), lambda b, g, c: (g, 0, 0)),
                  pl.BlockSpec((1, heads, 1), lambda b, g, c: (g, 0, 0)),
                  pl.BlockSpec((1, rp, q), lambda b, g, c: (g, 0, 0)),
                  pl.BlockSpec((1, rp, n_state), lambda b, g, c: (b, g, 0))],
        out_specs=[pl.BlockSpec((q, rp), lambda b, g, c: (b * nc + c, g)),
                   pl.BlockSpec((1, rp, n_state), lambda b, g, c: (b, g, 0))],
        out_shape=[jax.ShapeDtypeStruct((t, d_inner), BF16),
                   jax.ShapeDtypeStruct((nb, groups * rp, n_state), F32)],
        scratch_shapes=[pltpu.VMEM((rp, n_state), F32)],
        compiler_params=_cp("arbitrary", "arbitrary", "arbitrary"),
        name="ssd_prompt",
    )(xbc, xbc, xbc, dtg, dtg_t, a_row, a_col, dsk, s0)


def _ssd_sample_a_kernel(x_ref, b_ref, c_ref, dt_ref, a_ref, dsk_ref, e_ref,
                         yp_ref, ea_ref, xw_ref, cd_ref):
    steps = x_ref.shape[0]
    e01 = e_ref[...]
    a_row = a_ref[0]
    dts = [dt_ref[0, t] for t in range(steps)]
    acum = []
    for t in range(steps):
        da = dts[t] * a_row
        acum.append(da if t == 0 else acum[-1] + da)
    xs = [x_ref[t].astype(F32) for t in range(steps)]
    bs = [b_ref[t].astype(F32) for t in range(steps)]
    cs = [c_ref[t].astype(F32) for t in range(steps)]
    for t in range(steps):
        acc = dsk_ref[0] * xs[t]
        for j in range(t + 1):
            gsc = jnp.sum(cs[t] * bs[j], axis=-1, keepdims=True)
            wgt = gsc * jnp.exp(acum[t] - acum[j]) * dts[j]
            acc = acc + _dot01_right(wgt, e01) * xs[j]
        yp_ref[t] = acc
        ea_ref[t] = _dot01_right(jnp.exp(acum[t]), e01)
        xw_ref[t] = (_dot01_right(dts[t] * jnp.exp(acum[steps - 1] - acum[t]), e01) * xs[t]).astype(xw_ref.dtype)
    cd_ref[0] = jnp.exp(acum[steps - 1])


def _ssd_sample_a(xbc3, dt, a_log, d_skip, groups, heads, hdim, n_state):
    steps, nb, _ = xbc3.shape
    rp = heads * hdim
    d_inner = groups * rp
    a = -jnp.exp(a_log.astype(F32)).reshape(groups, 1, heads)
    dtg = dt.reshape(steps, nb, groups, heads).transpose(2, 0, 1, 3)
    dsk = jnp.repeat(d_skip.astype(F32), hdim).reshape(groups, 1, rp)
    e01 = (jnp.arange(rp)[None, :] // hdim == jnp.arange(heads)[:, None]).astype(BF16)
    b_off = d_inner // n_state
    c_off = (d_inner + groups * n_state) // n_state
    big = pl.BlockSpec((steps, nb, rp), lambda g: (0, 0, g))
    return pl.pallas_call(
        _ssd_sample_a_kernel,
        grid=(groups,),
        in_specs=[big,
                  pl.BlockSpec((steps, nb, n_state), lambda g: (0, 0, b_off + g)),
                  pl.BlockSpec((steps, nb, n_state), lambda g: (0, 0, c_off + g)),
                  pl.BlockSpec((1, steps, nb, heads), lambda g: (g, 0, 0, 0)),
                  pl.BlockSpec((1, 1, heads), lambda g: (g, 0, 0)),
                  pl.BlockSpec((1, 1, rp), lambda g: (g, 0, 0)),
                  pl.BlockSpec((heads, rp), lambda g: (0, 0))],
        out_specs=[big, big, big, pl.BlockSpec((1, nb, heads), lambda g: (g, 0, 0))],
        out_shape=[jax.ShapeDtypeStruct((steps, nb, d_inner), F32),
                   jax.ShapeDtypeStruct((steps, nb, d_inner), F32),
                   jax.ShapeDtypeStruct((steps, nb, d_inner), BF16),
                   jax.ShapeDtypeStruct((groups, nb, heads), F32)],
        compiler_params=_cp("arbitrary"),
        name="ssd_sample_a",
    )(xbc3, xbc3, xbc3, dtg, a, dsk, e01)


def _ssd_sample_b_kernel(s_ref, c_ref, b_ref, xw_ref, ea_ref, yp_ref, cd_ref, y_ref, so_ref,
                         *, groups, heads, hdim):
    rp = heads * hdim
    n_state = s_ref.shape[2]
    rows8 = yp_ref.shape[1]
    cdl = jnp.broadcast_to(cd_ref[0], (groups * heads, n_state))
    for g in range(groups):
        s0 = s_ref[0, g * rp:(g + 1) * rp, :]
        cg = c_ref[0, :, g * n_state:(g + 1) * n_state]
        bg = b_ref[0, :, g * n_state:(g + 1) * n_state]
        yoff = lax.dot_general(cg, s0.astype(BF16), NT_DIMS, preferred_element_type=F32)
        cols = slice(g * rp, (g + 1) * rp)
        y_ref[0, :, cols] = yp_ref[0, :, cols] + ea_ref[0, :, cols] * yoff[:rows8]
        upd = lax.dot_general(xw_ref[0, :, cols], bg, TN_DIMS, preferred_element_type=F32)
        cdrows = jnp.concatenate(
            [jnp.broadcast_to(cdl[g * heads + r:g * heads + r + 1, :], (hdim, n_state))
             for r in range(heads)], axis=0)
        so_ref[0, g * rp:(g + 1) * rp, :] = cdrows * s0 + upd


def _ssd_sample_b(s0, c_b, b_b, xw_b, ea_b, yp_b, cd_col, groups, heads, hdim):
    nb, hp, n_state = s0.shape
    d_inner = hp
    r16 = c_b.shape[1]
    r8 = yp_b.shape[1]
    gn = c_b.shape[2]
    return pl.pallas_call(
        functools.partial(_ssd_sample_b_kernel, groups=groups, heads=heads, hdim=hdim),
        grid=(nb,),
        in_specs=[pl.BlockSpec((1, hp, n_state), lambda b: (b, 0, 0)),
                  pl.BlockSpec((1, r16, gn), lambda b: (b, 0, 0)),
                  pl.BlockSpec((1, r16, gn), lambda b: (b, 0, 0)),
                  pl.BlockSpec((1, r16, d_inner), lambda b: (b, 0, 0)),
                  pl.BlockSpec((1, r8, d_inner), lambda b: (b, 0, 0)),
                  pl.BlockSpec((1, r8, d_inner), lambda b: (b, 0, 0)),
                  pl.BlockSpec((1, groups * heads, 1), lambda b: (b, 0, 0))],
        out_specs=[pl.BlockSpec((1, r8, d_inner), lambda b: (b, 0, 0)),
                   pl.BlockSpec((1, hp, n_state), lambda b: (b, 0, 0))],
        out_shape=[jax.ShapeDtypeStruct((nb, r8, d_inner), F32),
                   jax.ShapeDtypeStruct((nb, hp, n_state), F32)],
        compiler_params=_cp("arbitrary"),
        name="ssd_sample_b",
    )(s0, c_b, b_b, xw_b, ea_b, yp_b, cd_col)


def _gatednorm_kernel(y_ref, z_ref, g_ref, o_ref):
    v = y_ref[...].astype(F32) * _silu(z_ref[...].astype(F32))
    o_ref[...] = (v * lax.rsqrt(jnp.mean(v * v, axis=-1, keepdims=True) + EPS) * g_ref[...]).astype(o_ref.dtype)


def _gatednorm(y, proj, z_off, g, tm):
    t, d = y.shape
    assert z_off % d == 0
    zo = z_off // d
    return pl.pallas_call(
        _gatednorm_kernel,
        grid=(t // tm,),
        in_specs=[pl.BlockSpec((tm, d), lambda i: (i, 0)),
                  pl.BlockSpec((tm, d), lambda i: (i, zo)),
                  pl.BlockSpec((1, d), lambda i: (0, 0))],
        out_specs=pl.BlockSpec((tm, d), lambda i: (i, 0)),
        out_shape=jax.ShapeDtypeStruct((t, d), BF16),
        compiler_params=_cp("arbitrary"),
        name="gatednorm",
    )(y, proj, g.reshape(1, d))


CONF_HALO = 32


def _layernorm_silu(v, g, b):
    mu = jnp.mean(v, axis=-1, keepdims=True)
    vc = v - mu
    var = jnp.mean(vc * vc, axis=-1, keepdims=True)
    return _silu(vc * lax.rsqrt(var + EPS) * g + b)


def _conf_prompt_kernel(ca_ref, cg_ref, hb_ref, w_ref, b_ref, g_ref, lb_ref, o_ref, tail_ref,
                        u_scr, v_scr, *, kc, lane_chunk):
    li = pl.program_id(1)
    tl = ca_ref.shape[1]
    d = ca_ref.shape[2]

    @pl.when(li == 0)
    def _():
        u_scr[0:CONF_HALO, :] = hb_ref[0]

    @pl.when(li > 0)
    def _():
        u_scr[0:CONF_HALO, :] = u_scr[tl:tl + CONF_HALO, :]

    u_scr[CONF_HALO:, :] = ca_ref[0].astype(F32) * jax.nn.sigmoid(cg_ref[0].astype(F32))

    def chunk(ci, carry):
        lanes = pl.ds(pl.multiple_of(ci * lane_chunk, lane_chunk), lane_chunk)
        acc = jnp.broadcast_to(b_ref[:, lanes], (tl, lane_chunk))
        for k in range(kc):
            off = CONF_HALO - (kc - 1) + k
            acc = acc + w_ref[k:k + 1, lanes] * u_scr[off:off + tl, lanes]
        v_scr[:, lanes] = acc
        return carry

    lax.fori_loop(0, d // lane_chunk, chunk, 0)
    o_ref[...] = _layernorm_silu(v_scr[...], g_ref[...], lb_ref[...]).astype(o_ref.dtype)

    @pl.when(li == pl.num_programs(1) - 1)
    def _():
        tail_ref[0] = u_scr[tl:tl + CONF_HALO, :]


def _conf_prompt(proj3, hb, w, b, ln_g, ln_b, ca_off, cg_off):
    nb, seq, _ = proj3.shape
    kc, d = w.shape
    tl = _tile(seq, 128, CONF_HALO)
    nl = seq // tl
    assert ca_off % d == 0 and cg_off % d == 0 and tl >= CONF_HALO and kc - 1 <= CONF_HALO
    cao, cgo = ca_off // d, cg_off // d
    lane_chunk = _tile(d, 512)
    row = lambda bi, li: (0, 0)
    return pl.pallas_call(
        functools.partial(_conf_prompt_kernel, kc=kc, lane_chunk=lane_chunk),
        grid=(nb, nl),
        in_specs=[pl.BlockSpec((1, tl, d), lambda bi, li: (bi, li, cao)),
                  pl.BlockSpec((1, tl, d), lambda bi, li: (bi, li, cgo)),
                  pl.BlockSpec((1, CONF_HALO, d), lambda bi, li: (bi, 0, 0)),
                  pl.BlockSpec((kc, d), row),
                  pl.BlockSpec((1, d), row), pl.BlockSpec((1, d), row), pl.BlockSpec((1, d), row)],
        out_specs=[pl.BlockSpec((tl, d), lambda bi, li: (bi * nl + li, 0)),
                   pl.BlockSpec((1, CONF_HALO, d), lambda bi, li: (bi, 0, 0))],
        out_shape=[jax.ShapeDtypeStruct((nb * seq, d), BF16),
                   jax.ShapeDtypeStruct((nb, CONF_HALO, d), F32)],
        scratch_shapes=[pltpu.VMEM((tl + CONF_HALO, d), F32), pltpu.VMEM((tl, d), F32)],
        compiler_params=_cp("arbitrary", "arbitrary"),
        name="conf_prompt",
    )(proj3, proj3, hb, w, b.reshape(1, d), ln_g.reshape(1, d), ln_b.reshape(1, d))


def _conf_sample_kernel(ca_ref, cg_ref, buf_ref, w_ref, b_ref, g_ref, lb_ref, o_ref, uc_ref, *, kc):
    steps = ca_ref.shape[0]
    ucs = []
    for t in range(steps):
        uc = ca_ref[t].astype(F32) * jax.nn.sigmoid(cg_ref[t].astype(F32))
        uc_ref[t] = uc
        ucs.append(uc)
    for t in range(steps):
        acc = b_ref[...]
        for k in range(kc):
            m = t + k
            slab = buf_ref[m] if m < kc - 1 else ucs[m - (kc - 1)]
            acc = acc + w_ref[k:k + 1, :] * slab
        o_ref[t] = _layernorm_silu(acc, g_ref[...], lb_ref[...]).astype(o_ref.dtype)


def _conf_sample(proj3, buf_t, w, b, ln_g, ln_b, ca_off, cg_off):
    steps, nb, _ = proj3.shape
    kc, d = w.shape
    bb = SUBLANES
    cao, cgo = ca_off // d, cg_off // d
    row = lambda i: (0, 0)
    return pl.pallas_call(
        functools.partial(_conf_sample_kernel, kc=kc),
        grid=(nb // bb,),
        in_specs=[pl.BlockSpec((steps, bb, d), lambda i: (0, i, cao)),
                  pl.BlockSpec((steps, bb, d), lambda i: (0, i, cgo)),
                  pl.BlockSpec((kc - 1, bb, d), lambda i: (0, i, 0)),
                  pl.BlockSpec((kc, d), row),
                  pl.BlockSpec((1, d), row), pl.BlockSpec((1, d), row), pl.BlockSpec((1, d), row)],
        out_specs=[pl.BlockSpec((steps, bb, d), lambda i: (0, i, 0)),
                   pl.BlockSpec((steps, bb, d), lambda i: (0, i, 0))],
        out_shape=[jax.ShapeDtypeStruct((steps, nb, d), BF16),
                   jax.ShapeDtypeStruct((steps, nb, d), F32)],
        compiler_params=_cp("arbitrary"),
        name="conf_sample",
    )(proj3, proj3, buf_t, w, b.reshape(1, d), ln_g.reshape(1, d), ln_b.reshape(1, d))


def _top_values(s, k, scr):
    cur = s
    for r in range(k):
        m = jnp.max(cur, axis=0, keepdims=True)
        scr[r:r + 1, :] = m
        cur = jnp.where(cur >= m, NEG, cur)
    return scr[...]


def _peer_topk_kernel(q_ref, k1_ref, k2_ref, thr_ref, e1_ref, s2_ref, e2_ref, v1_scr, v2_scr,
                      *, n_heads, half, topk):
    for h in range(n_heads):
        base = h * 2 * half
        q1 = q_ref[:, base:base + half].astype(BF16)
        q2 = q_ref[:, base + half:base + 2 * half].astype(BF16)
        s1 = lax.dot_general(k1_ref[h].astype(BF16), q1, NT_DIMS, preferred_element_type=F32)
        s2 = lax.dot_general(k2_ref[h].astype(BF16), q2, NT_DIMS, preferred_element_type=F32)
        v1 = _top_values(s1, topk, v1_scr)
        v2 = _top_values(s2, topk, v2_scr)
        cand = jnp.concatenate([v1[a:a + 1, :] + v2 for a in range(topk)], axis=0)
        cur = cand
        tau = None
        for r in range(topk):
            tau = jnp.max(cur, axis=0, keepdims=True)
            if r < topk - 1:
                cur = jnp.where(cur >= tau, NEG, cur)
        top = v1[0:1, :] + v2[0:1, :]
        z = jnp.sum(jnp.where(cand >= tau, jnp.exp(cand - top), 0.0), axis=0, keepdims=True)
        thr_ref[h] = tau - s1
        e1_ref[h] = jnp.exp(s1 - v1[0:1, :]) / z
        s2_ref[h] = s2
        e2_ref[h] = jnp.exp(s2 - v2[0:1, :])


def _peer_topk(qv, keys1, keys2):
    t, _ = qv.shape
    n_heads, nk, half = keys1.shape
    tq = _tile(t, 256)
    kspec = pl.BlockSpec((n_heads, nk, half), lambda i: (0, 0, 0))
    ospec = pl.BlockSpec((n_heads, nk, tq), lambda i: (0, 0, i))
    oshape = jax.ShapeDtypeStruct((n_heads, nk, t), F32)
    return pl.pallas_call(
        functools.partial(_peer_topk_kernel, n_heads=n_heads, half=half, topk=PEER_TOPK),
        grid=(t // tq,),
        in_specs=[pl.BlockSpec((tq, n_heads * 2 * half), lambda i: (i, 0)), kspec, kspec],
        out_specs=[ospec] * 4,
        out_shape=[oshape] * 4,
        scratch_shapes=[pltpu.VMEM((PEER_TOPK, tq), F32), pltpu.VMEM((PEER_TOPK, tq), F32)],
        compiler_params=_cp("arbitrary"),
        name="peer_topk",
    )(qv, keys1, keys2)


def _gelu_exact(x):
    return 0.5 * x * (1.0 + lax.erf(x * 0.7071067811865476))


def _peer_dense_kernel(h_ref, u_ref, v_ref, thr_ref, e1_ref, s2_ref, e2_ref, o_ref, *, n_heads, nk):
    j = pl.program_id(1)
    te = u_ref.shape[0]
    tm = h_ref.shape[0]

    @pl.when(j == 0)
    def _():
        o_ref[...] = jnp.zeros_like(o_ref)

    act = _gelu_exact(lax.dot_general(u_ref[...], h_ref[...], NT_DIMS, preferred_element_type=F32))
    parts = []
    for a in range(te // nk):
        i1 = j * (te // nk) + a
        w = jnp.zeros((nk, tm), F32)
        for h in range(n_heads):
            thr = thr_ref[h, pl.ds(i1, 1), :]
            e1 = e1_ref[h, pl.ds(i1, 1), :]
            w = w + jnp.where(s2_ref[h] >= thr, e2_ref[h] * e1, 0.0)
        parts.append(w)
    gate_t = parts[0] if len(parts) == 1 else jnp.concatenate(parts, axis=0)
    a = (act * gate_t).T.astype(BF16)
    d = o_ref.shape[1]
    dc = _tile(d, 1024)
    for c in range(d // dc):
        cols = slice(c * dc, (c + 1) * dc)
        o_ref[:, cols] += jnp.dot(a, v_ref[:, cols], preferred_element_type=F32)


def _peer_dense(h2, u, v, thr, e1, s2, e2):
    t, d = h2.shape
    n_exp = u.shape[0]
    n_heads, nk, _ = thr.shape
    tm = _tile(t, 512)
    te = 2 * nk
    fspec = pl.BlockSpec((n_heads, nk, tm), lambda i, j: (0, 0, i))
    return pl.pallas_call(
        functools.partial(_peer_dense_kernel, n_heads=n_heads, nk=nk),
        grid=(t // tm, n_exp // te),
        in_specs=[pl.BlockSpec((tm, d), lambda i, j: (i, 0)),
                  pl.BlockSpec((te, d), lambda i, j: (j, 0)),
                  pl.BlockSpec((te, d), lambda i, j: (j, 0)),
                  fspec, fspec, fspec, fspec],
        out_specs=pl.BlockSpec((tm, d), lambda i, j: (i, 0)),
        out_shape=jax.ShapeDtypeStruct((t, d), F32),
        compiler_params=_cp("arbitrary", "arbitrary"),
        name="peer_dense",
    )(h2, u, v, thr, e1, s2, e2)


def _residual_kernel(x_ref, p_ref, gate_ref, g_ref, o_ref, *, final_norm):
    x = x_ref[...] + gate_ref[0] * p_ref[...]
    if final_norm:
        x = x * lax.rsqrt(jnp.mean(x * x, axis=-1, keepdims=True) + EPS) * g_ref[...]
    o_ref[...] = x


def _residual(x1, pe, grp, k_gate, g_final, final_norm):
    t, d = x1.shape
    tm = grp.row_tile(256)
    return pl.pallas_call(
        functools.partial(_residual_kernel, final_norm=final_norm),
        grid=(t // tm, 1),
        in_specs=[pl.BlockSpec((tm, d), lambda i, j: (i, 0)),
                  pl.BlockSpec((tm, d), lambda i, j: (i, 0)),
                  grp.mod_spec(tm, d),
                  pl.BlockSpec((1, d), lambda i, j: (0, 0))],
        out_specs=pl.BlockSpec((tm, d), lambda i, j: (i, 0)),
        out_shape=jax.ShapeDtypeStruct((t, d), F32),
        compiler_params=_cp("arbitrary", "arbitrary"),
        name="residual",
    )(x1, pe, grp.mods[k_gate], g_final.reshape(1, d))


def _softplus(x):
    return jnp.maximum(x, 0.0) + jnp.log1p(jnp.exp(-jnp.abs(x)))


def _layer_group(x, grp, prm, dims, ssd_fn, conv_fn, conf_fn, final_g, final_norm):
    d = dims["d"]
    d_inner = dims["d_inner"]
    off = dims["off"]
    tm = grp.row_tile(1024)
    tn = 1024

    h = _modnorm(x, prm["norm1_g"], grp, 1, 0)
    proj = _mm(h, prm["w_main"], tm=tm, tn=_tile(prm["w_main"].shape[1], tn), out_dtype=BF16, name="in_proj")
    hs = prm["w_dt"].shape[1]
    dt = _mm(h, prm["w_dt"], tm=tm, tn=hs, out_dtype=F32,
             epi=lambda acc, bias: _softplus(acc + bias),
             extras=[(prm["dt_bias"].reshape(1, hs), _row_spec(hs))], name="dt_proj")

    xbc = conv_fn(proj)
    y, ssm_new = ssd_fn(xbc, dt)
    yn = _gatednorm(y, proj, off["z"], prm["ssd_norm_g"], grp.row_tile(256))
    tn_d = _tile(d, 512)
    sa = _mm(yn, prm["w_ssd_out"], tm=grp.row_tile(512), tn=tn_d, out_dtype=F32,
             epi=lambda acc, ga: jax.nn.sigmoid(ga.astype(F32)) * acc,
             extras=[(proj, _tile_spec(grp.row_tile(512), tn_d, off["ga"]))], name="ssd_out")

    vc, conf_state = conf_fn(proj)
    tmm = grp.row_tile(512)
    tn_m = _tile(d, 1024)
    mixed = _mm(vc, prm["w_conf_out"], tm=tmm, tn=tn_m, out_dtype=BF16,
                epi=lambda acc, s, gb: s + jax.nn.sigmoid(gb.astype(F32)) * acc,
                extras=[(sa, _tile_spec(tmm, tn_m, 0)), (proj, _tile_spec(tmm, tn_m, off["gb"]))],
                name="conf_out")
    x1 = _mm(mixed, prm["w_out"], tm=tmm, tn=tn_m, out_dtype=F32,
             epi=lambda acc, xr, gate: xr + gate * acc,
             extras=[(x, _tile_spec(tmm, tn_m, 0)), (grp.mods[2], grp.mod_spec(tmm, tn_m))],
             name="out_proj")

    h2 = _modnorm(x1, prm["norm2_g"], grp, 4, 3)
    qv = _mm(h2, prm["peer_wq"], tm=tmm, tn=_tile(prm["peer_wq"].shape[1], 1024), out_dtype=F32, name="peer_q")
    thr, e1, s2, e2 = _peer_topk(qv, prm["peer_keys1"], prm["peer_keys2"])
    pe = _peer_dense(h2, prm["peer_u"], prm["peer_v"], thr, e1, s2, e2)
    x2 = _residual(x1, pe, grp, 5, final_g, final_norm)
    return x2, proj, ssm_new, conf_state


def kernel(x_prompt, x_sample, c_prompt, c_sample, state_ssm, state_ssd_conv, state_conf_conv, w_ada, b_ada, norm1_g, w_in, ssd_conv_w, ssd_conv_b, dt_bias, a_log, d_skip, ssd_norm_g, w_ssd_out, conf_dw_w, conf_dw_b, conf_ln_g, conf_ln_b, w_conf_out, w_out, norm2_g, peer_wq, peer_keys1, peer_keys2, peer_u, peer_v, final_norm_g):
    depth = w_ada.shape[0]
    bp, lp, d = x_prompt.shape
    bs, ls, _ = x_sample.shape
    n_state = state_ssm.shape[-1]
    hdim = state_ssm.shape[-2]
    n_ssm_heads = a_log.shape[-1]
    d_inner = ssd_norm_g.shape[-1]
    conv_dim = ssd_conv_w.shape[-1]
    groups = (conv_dim - d_inner) // (2 * n_state)
    heads = n_ssm_heads // groups
    d_conf = conf_dw_w.shape[-1]
    kc = conf_dw_w.shape[-2]
    ks = ssd_conv_w.shape[-2]
    n_mod = w_ada.shape[-1] // d

    c_z, c_xbc, c_dt = d_inner, d_inner + conv_dim, d_inner + conv_dim + n_ssm_heads
    off = {"z": 0, "ca": d_inner, "cg": d_inner + d_conf, "ga": d_inner + 2 * d_conf,
           "gb": d_inner + 2 * d_conf + d, "xbc": d_inner + 2 * d_conf + 2 * d}
    dims = {"d": d, "d_inner": d_inner, "off": off}

    xp = x_prompt.reshape(bp * lp, d)
    xs = x_sample.transpose(1, 0, 2).reshape(ls * bs, d)
    rows_c = bp + bs
    rows_pad = -(-rows_c // SUBLANES) * SUBLANES
    c_all = jnp.pad(jnp.concatenate([c_prompt, c_sample], axis=0), ((0, rows_pad - rows_c), (0, 0)))

    outs = {k: [] for k in ("ssm_p", "sconv_p", "cconv_p", "ssm_s", "sconv_s", "cconv_s")}
    for li in range(depth):
        final_norm = li == depth - 1
        mod = _adaln(c_all, w_ada[li], b_ada[li])
        mod_p = mod[:bp].reshape(bp, n_mod, d)
        mod_s = mod[bp:bp + bs].reshape(bs, n_mod, d)
        grp_p = _Group(bp * lp, lp, False, [mod_p[:, k][:, None, :] for k in range(n_mod)])
        grp_s = _Group(ls * bs, ls, True, [jnp.tile(mod_s[:, k], (ls, 1))[None] for k in range(n_mod)])

        wi = w_in[li]
        prm = {
            "norm1_g": norm1_g[li], "norm2_g": norm2_g[li], "ssd_norm_g": ssd_norm_g[li],
            "w_main": jnp.concatenate([wi[:, :c_z], wi[:, c_dt:], wi[:, c_z:c_xbc]], axis=1).astype(BF16),
            "w_dt": wi[:, c_xbc:c_dt].astype(BF16), "dt_bias": dt_bias[li].astype(F32),
            "w_ssd_out": w_ssd_out[li].astype(BF16), "w_conf_out": w_conf_out[li].astype(BF16),
            "w_out": w_out[li].astype(BF16), "peer_wq": peer_wq[li].astype(BF16),
            "peer_keys1": peer_keys1[li], "peer_keys2": peer_keys2[li],
            "peer_u": peer_u[li].astype(BF16), "peer_v": peer_v[li].astype(BF16),
        }
        ncols = prm["w_main"].shape[1]

        def conv_p(proj):
            hb = jnp.zeros((bp, SUBLANES, conv_dim), F32)
            return _ssdconv_prompt(proj.reshape(bp, lp, ncols), hb, ssd_conv_w[li], ssd_conv_b[li],
                                   off["xbc"], BF16).reshape(bp * lp, conv_dim)

        def ssd_p(xbc, dt):
            s0 = jnp.zeros((bp, n_ssm_heads * hdim, n_state), F32)
            return _ssd_prompt(xbc, dt, a_log[li], d_skip[li], s0, bp, lp, groups, heads, hdim, n_state)

        def conf_p(proj):
            hb = jnp.zeros((bp, CONF_HALO, d_conf), F32)
            return _conf_prompt(proj.reshape(bp, lp, ncols), hb, conf_dw_w[li], conf_dw_b[li],
                                conf_ln_g[li], conf_ln_b[li], off["ca"], off["cg"])

        xp, proj_p, ssm_p, tail_p = _layer_group(xp, grp_p, prm, dims, ssd_p, conv_p, conf_p,
                                                 final_norm_g, final_norm)
        outs["ssm_p"].append(ssm_p.reshape(bp, n_ssm_heads, hdim, n_state))
        xbc_raw_p = proj_p.reshape(bp, lp, ncols)[:, :, off["xbc"]:].astype(F32)
        sconv0 = jnp.zeros((bp, ks - 1, conv_dim), F32)
        outs["sconv_p"].append(jnp.concatenate([sconv0, xbc_raw_p[:, -(ks - 1):]], axis=1)[:, -(ks - 1):])
        outs["cconv_p"].append(tail_p[:, CONF_HALO - (kc - 1):])

        sbuf = state_ssd_conv[li]
        cbuf = state_conf_conv[li]
        ssm0 = state_ssm[li].reshape(bs, n_ssm_heads * hdim, n_state)

        def conv_s(proj):
            return _ssdconv_sample(proj.reshape(ls, bs, ncols), sbuf.transpose(1, 0, 2), ssd_conv_w[li],
                                   ssd_conv_b[li], off["xbc"], BF16).reshape(ls * bs, conv_dim)

        def ssd_s(xbc, dt):
            xbc3 = xbc.reshape(ls, bs, conv_dim)
            yp, ea, xw, cd = _ssd_sample_a(xbc3, dt, a_log[li], d_skip[li], groups, heads, hdim, n_state)
            gn = groups * n_state

            def bmajor(v, rows):
                return jnp.pad(v.transpose(1, 0, 2), ((0, 0), (0, rows - ls), (0, 0)))

            r16 = -(-ls // 16) * 16
            r8 = -(-ls // SUBLANES) * SUBLANES
            b_b = bmajor(xbc3[:, :, d_inner:d_inner + gn], r16)
            c_b = bmajor(xbc3[:, :, d_inner + gn:], r16)
            cd_col = cd.transpose(1, 0, 2).reshape(bs, groups * heads, 1)
            y_b, s_new = _ssd_sample_b(ssm0, c_b, b_b, bmajor(xw, r16), bmajor(ea, r8), bmajor(yp, r8),
                                       cd_col, groups, heads, hdim)
            y = y_b[:, :ls].transpose(1, 0, 2).reshape(ls * bs, d_inner).astype(BF16)
            return y, s_new

        def conf_s(proj):
            vc, uc = _conf_sample(proj.reshape(ls, bs, ncols), cbuf.transpose(1, 0, 2), conf_dw_w[li],
                                  conf_dw_b[li], conf_ln_g[li], conf_ln_b[li], off["ca"], off["cg"])
            return vc.reshape(ls * bs, d_conf), uc

        xs, proj_s, ssm_s, uc_s = _layer_group(xs, grp_s, prm, dims, ssd_s, conv_s, conf_s,
                                               final_norm_g, final_norm)
        outs["ssm_s"].append(ssm_s.reshape(bs, n_ssm_heads, hdim, n_state))
        xbc_raw_s = proj_s.reshape(ls, bs, ncols)[:, :, off["xbc"]:].astype(F32).transpose(1, 0, 2)
        outs["sconv_s"].append(jnp.concatenate([sbuf, xbc_raw_s], axis=1)[:, -(ks - 1):])
        outs["cconv_s"].append(jnp.concatenate([cbuf, uc_s.transpose(1, 0, 2)], axis=1)[:, -(kc - 1):])

    y_prompt = xp.reshape(bp, lp, d)
    y_sample = xs.reshape(ls, bs, d).transpose(1, 0, 2)
    return (y_prompt, y_sample,
            jnp.stack(outs["ssm_p"], 0), jnp.stack(outs["sconv_p"], 0), jnp.stack(outs["cconv_p"], 0),
            jnp.stack(outs["ssm_s"], 0), jnp.stack(outs["sconv_s"], 0), jnp.stack(outs["cconv_s"], 0))
```

```python
import functools

import jax
import jax.numpy as jnp
from jax import lax
from jax.experimental import pallas as pl
from jax.experimental.pallas import tpu as pltpu

F32 = jnp.float32
BF16 = jnp.bfloat16
EPS = 1e-6
PEER_TOPK = 16
SSD_CHUNK = 128
NEG = -3.0e38
V7X_VMEM_LIMIT_BYTES = 56 * 1024 * 1024
SUBLANES = 8

NT_DIMS = (((1,), (1,)), ((), ()))
TN_DIMS = (((0,), (0,)), ((), ()))


def _cp(*sem):
    return pltpu.CompilerParams(dimension_semantics=sem,
                                vmem_limit_bytes=V7X_VMEM_LIMIT_BYTES)


def _tile(n, pref, mult=128):
    best = None
    t = mult
    while t <= min(n, pref):
        if n % t == 0:
            best = t
        t += mult
    return best if best is not None else n


def _silu(x):
    return x * jax.nn.sigmoid(x)


def _split3(a):
    hi = a.astype(BF16)
    r = a - hi.astype(F32)
    mid = r.astype(BF16)
    lo = (r - mid.astype(F32)).astype(BF16)
    return hi, mid, lo


def _dot01_left(m01, a):
    out = None
    for p in _split3(a):
        t = jnp.dot(m01, p, preferred_element_type=F32)
        out = t if out is None else out + t
    return out


def _dot01_right(a, m01):
    out = None
    for p in _split3(a):
        t = jnp.dot(p, m01, preferred_element_type=F32)
        out = t if out is None else out + t
    return out


class _Group:
    def __init__(self, n_tokens, seq_len, per_token, mods):
        self.T = n_tokens
        self.L = seq_len
        self.per_token = per_token
        self.mods = mods

    def mod_spec(self, tm, tn):
        if self.per_token:
            return pl.BlockSpec((1, tm, tn), lambda i, j: (0, i, j))
        tiles_per_batch = self.L // tm
        return pl.BlockSpec((1, 1, tn), lambda i, j: (i // tiles_per_batch, 0, j))

    def row_tile(self, pref):
        return _tile(self.T if self.per_token else self.L, pref, SUBLANES)


def _ada_kernel(c_ref, w_ref, b_ref, o_ref):
    a = _silu(c_ref[...]).astype(BF16)
    o_ref[...] = jnp.dot(a, w_ref[...].astype(BF16), preferred_element_type=F32) + b_ref[...]


def _adaln(c, w, b):
    m, d = c.shape
    n = w.shape[1]
    tn = _tile(n, 1024)
    return pl.pallas_call(
        _ada_kernel,
        grid=(n // tn,),
        in_specs=[pl.BlockSpec((m, d), lambda j: (0, 0)),
                  pl.BlockSpec((d, tn), lambda j: (0, j)),
                  pl.BlockSpec((1, tn), lambda j: (0, j))],
        out_specs=pl.BlockSpec((m, tn), lambda j: (0, j)),
        out_shape=jax.ShapeDtypeStruct((m, n), F32),
        compiler_params=_cp("arbitrary"),
        name="adaln",
    )(c, w, b.reshape(1, n))


def _modnorm_kernel(x_ref, g_ref, sc_ref, sh_ref, o_ref):
    x = x_ref[...]
    y = x * lax.rsqrt(jnp.mean(x * x, axis=-1, keepdims=True) + EPS) * g_ref[...]
    o_ref[...] = (y * (1.0 + sc_ref[0]) + sh_ref[0]).astype(o_ref.dtype)


def _modnorm(x, g, grp, k_scale, k_shift):
    t, d = x.shape
    tm = grp.row_tile(512)
    return pl.pallas_call(
        _modnorm_kernel,
        grid=(t // tm, 1),
        in_specs=[pl.BlockSpec((tm, d), lambda i, j: (i, 0)),
                  pl.BlockSpec((1, d), lambda i, j: (0, 0)),
                  grp.mod_spec(tm, d), grp.mod_spec(tm, d)],
        out_specs=pl.BlockSpec((tm, d), lambda i, j: (i, 0)),
        out_shape=jax.ShapeDtypeStruct((t, d), BF16),
        compiler_params=_cp("arbitrary", "arbitrary"),
        name="modnorm",
    )(x, g.reshape(1, d), grp.mods[k_scale], grp.mods[k_shift])


def _mm_kernel(a_ref, w_ref, *rest, epi, n_extra):
    extras = rest[:n_extra]
    o_ref = rest[n_extra]
    acc = jnp.dot(a_ref[...], w_ref[...], preferred_element_type=F32)
    vals = [e[0] if len(e.shape) == 3 else e[...] for e in extras]
    o_ref[...] = epi(acc, *vals).astype(o_ref.dtype)


def _mm(a, w, *, tm, tn, out_dtype, epi=None, extras=(), name="mm"):
    t, k = a.shape
    n = w.shape[1]
    if epi is None:
        epi = lambda acc: acc
    return pl.pallas_call(
        functools.partial(_mm_kernel, epi=epi, n_extra=len(extras)),
        grid=(t // tm, n // tn),
        in_specs=[pl.BlockSpec((tm, k), lambda i, j: (i, 0)),
                  pl.BlockSpec((k, tn), lambda i, j: (0, j))] + [s for _, s in extras],
        out_specs=pl.BlockSpec((tm, tn), lambda i, j: (i, j)),
        out_shape=jax.ShapeDtypeStruct((t, n), out_dtype),
        compiler_params=_cp("arbitrary", "arbitrary"),
        name=name,
    )(a, w, *[x for x, _ in extras])


def _tile_spec(tm, tn, col_off):
    assert col_off % tn == 0
    off = col_off // tn
    return pl.BlockSpec((tm, tn), lambda i, j: (i, j + off))


def _row_spec(tn):
    return pl.BlockSpec((1, tn), lambda i, j: (0, j))


def _ssdconv_prompt_kernel(x_ref, hb_ref, w_ref, b_ref, o_ref, u_scr, *, ks):
    seq = x_ref.shape[1]
    u_scr[0:SUBLANES, :] = hb_ref[0]
    u_scr[SUBLANES:, :] = x_ref[0].astype(F32)
    acc = b_ref[...]
    for k in range(ks):
        off = SUBLANES - (ks - 1) + k
        acc = acc + w_ref[k:k + 1, :] * u_scr[off:off + seq, :]
    o_ref[0] = _silu(acc).astype(o_ref.dtype)


def _ssdconv_prompt(proj3, hb, w, b, col_off, out_dtype):
    nb, seq, _ = proj3.shape
    ks, c = w.shape
    tc = _tile(c, 256)
    assert col_off % tc == 0
    off = col_off // tc
    return pl.pallas_call(
        functools.partial(_ssdconv_prompt_kernel, ks=ks),
        grid=(nb, c // tc),
        in_specs=[pl.BlockSpec((1, seq, tc), lambda bi, ci: (bi, 0, ci + off)),
                  pl.BlockSpec((1, SUBLANES, tc), lambda bi, ci: (bi, 0, ci)),
                  pl.BlockSpec((ks, tc), lambda bi, ci: (0, ci)),
                  pl.BlockSpec((1, tc), lambda bi, ci: (0, ci))],
        out_specs=pl.BlockSpec((1, seq, tc), lambda bi, ci: (bi, 0, ci)),
        out_shape=jax.ShapeDtypeStruct((nb, seq, c), out_dtype),
        scratch_shapes=[pltpu.VMEM((seq + SUBLANES, tc), F32)],
        compiler_params=_cp("arbitrary", "arbitrary"),
        name="ssdconv_prompt",
    )(proj3, hb, w, b.reshape(1, c))


def _ssdconv_sample_kernel(x_ref, buf_ref, w_ref, b_ref, o_ref, *, ks):
    steps = x_ref.shape[0]
    slabs = [buf_ref[m] for m in range(ks - 1)] + [x_ref[t].astype(F32) for t in range(steps)]
    for t in range(steps):
        acc = b_ref[...]
        for k in range(ks):
            acc = acc + w_ref[k:k + 1, :] * slabs[t + k]
        o_ref[t] = _silu(acc).astype(o_ref.dtype)


def _ssdconv_sample(proj3, buf_t, w, b, col_off, out_dtype):
    steps, nb, _ = proj3.shape
    ks, c = w.shape
    tc = _tile(c, 512)
    assert col_off % tc == 0
    off = col_off // tc
    return pl.pallas_call(
        functools.partial(_ssdconv_sample_kernel, ks=ks),
        grid=(c // tc,),
        in_specs=[pl.BlockSpec((steps, nb, tc), lambda ci: (0, 0, ci + off)),
                  pl.BlockSpec((ks - 1, nb, tc), lambda ci: (0, 0, ci)),
                  pl.BlockSpec((ks, tc), lambda ci: (0, ci)),
                  pl.BlockSpec((1, tc), lambda ci: (0, ci))],
        out_specs=pl.BlockSpec((steps, nb, tc), lambda ci: (0, 0, ci)),
        out_shape=jax.ShapeDtypeStruct((steps, nb, c), out_dtype),
        compiler_params=_cp("arbitrary"),
        name="ssdconv_sample",
    )(proj3, buf_t, w, b.reshape(1, c))


def _ssd_prompt_kernel(x_ref, b_ref, c_ref, dt_ref, dtT_ref, arow_ref, acol_ref, dsk_ref, s0_ref,
                       y_ref, sfin_ref, s_scr, *, heads, hdim, gb):
    ci = pl.program_id(2)
    q = x_ref.shape[0]
    n_state = b_ref.shape[1] // gb
    rp = heads * hdim

    @pl.when(ci == 0)
    def _():
        s_scr[...] = s0_ref[0]

    ii = lax.broadcasted_iota(jnp.int32, (q, q), 0)
    jj = lax.broadcasted_iota(jnp.int32, (q, q), 1)
    lower = jnp.where(ii >= jj, 1.0, 0.0).astype(BF16)
    upper = jnp.where(ii <= jj, 1.0, 0.0).astype(BF16)
    causal = ii <= jj

    def rows(v):
        return jnp.concatenate(
            [jnp.broadcast_to(v[r:r + 1, :], (hdim, v.shape[1])) for r in range(heads)], axis=0)

    for gi in range(gb):
        ch = slice(gi * rp, (gi + 1) * rp)
        st = slice(gi * n_state, (gi + 1) * n_state)
        dt = dt_ref[gi]
        dt_t = dtT_ref[gi]
        acum = _dot01_left(lower, dt * arow_ref[gi])
        acum_t = _dot01_right(dt_t * acol_ref[gi], upper)
        x_t = x_ref[:, ch].astype(F32).T
        bm = b_ref[:, st]
        cm = c_ref[:, st]
        scores_t = lax.dot_general(bm, cm, NT_DIMS, preferred_element_type=F32)
        last = acum_t[:, q - 1:q]
        xdt_t = (x_t * rows(dt_t)).astype(BF16)
        ydiag = []
        for r in range(heads):
            seg = acum_t[r:r + 1, :] - acum[:, r:r + 1]
            dec = jnp.exp(jnp.where(causal, seg, NEG))
            wgt = (scores_t * dec).astype(BF16)
            ydiag.append(jnp.dot(xdt_t[r * hdim:(r + 1) * hdim, :], wgt, preferred_element_type=F32))
        s_prev = s_scr[ch, :]
        yoff_t = lax.dot_general(s_prev.astype(BF16), cm, NT_DIMS, preferred_element_type=F32)
        y_t = jnp.concatenate(ydiag, axis=0) + yoff_t * rows(jnp.exp(acum_t)) + dsk_ref[gi] * x_t
        y_ref[:, ch] = y_t.T.astype(y_ref.dtype)

        coef_t = dt_t * jnp.exp(last - acum_t)
        chunk_s = jnp.dot((x_t * rows(coef_t)).astype(BF16), bm, preferred_element_type=F32)
        cd = jnp.broadcast_to(jnp.exp(last), (heads, n_state))
        s_scr[ch, :] = rows(cd) * s_prev + chunk_s

    @pl.when(ci == pl.num_programs(2) - 1)
    def _():
        sfin_ref[0] = s_scr[...]


def _ssd_prompt(xbc, dt, a_log, d_skip, s0, nb, seq, groups, heads, hdim, n_state):
    t = xbc.shape[0]
    q = SSD_CHUNK if seq % SSD_CHUNK == 0 else seq
    nc = seq // q
    rp = heads * hdim
    d_inner = groups * rp
    a = -jnp.exp(a_log.astype(F32))
    dtg = dt.reshape(t, groups, heads).transpose(1, 0, 2)
    dtg_t = dt.reshape(t, groups, heads).transpose(1, 2, 0)
    a_row = a.reshape(groups, 1, heads)
    a_col = a.reshape(groups, heads, 1)
    dsk = jnp.broadcast_to(jnp.repeat(d_skip.astype(F32), hdim).reshape(groups, rp, 1), (groups, rp, q))
    gb = 2 if groups % 2 == 0 else 1
    ng = groups // gb
    gn = gb * n_state
    assert d_inner % gn == 0 and (groups * n_state) % gn == 0
    b_off = d_inner // gn
    c_off = (d_inner + groups * n_state) // gn
    return pl.pallas_call(
        functools.partial(_ssd_prompt_kernel, heads=heads, hdim=hdim, gb=gb),
        grid=(nb, ng, nc),
        in_specs=[pl.BlockSpec((q, gb * rp), lambda b, g, c: (b * nc + c, g)),
                  pl.BlockSpec((q, gn), lambda b, g, c: (b * nc + c, b_off + g)),
                  pl.BlockSpec((q, gn), lambda b, g, c: (b * nc + c, c_off + g)),
                  pl.BlockSpec((gb, q, heads), lambda b, g, c: (g, b * nc + c, 0)),
                  pl.BlockSpec((gb, heads, q), lambda b, g, c: (g, 0, b * nc + c)),
                  pl.BlockSpec((gb, 1, heads), lambda b, g, c: (g, 0, 0)),
                  pl.BlockSpec((gb, heads, 1), lambda b, g, c: (g, 0, 0)),
                  pl.BlockSpec((gb, rp, q), lambda b, g, c: (g, 0, 0)),
                  pl.BlockSpec((1, gb * rp, n_state), lambda b, g, c: (b, g, 0))],
        out_specs=[pl.BlockSpec((q, gb * rp), lambda b, g, c: (b * nc + c, g)),
                   pl.BlockSpec((1, gb * rp, n_state), lambda b, g, c: (b, g, 0))],
        out_shape=[jax.ShapeDtypeStruct((t, d_inner), BF16),
                   jax.ShapeDtypeStruct((nb, groups * rp, n_state), F32)],
        scratch_shapes=[pltpu.VMEM((gb * rp, n_state), F32)],
        compiler_params=_cp("arbitrary", "arbitrary", "arbitrary"),
        name="ssd_prompt",
    )(xbc, xbc, xbc, dtg, dtg_t, a_row, a_col, dsk, s0)


def _ssd_sample_a_kernel(x_ref, b_ref, c_ref, dt_ref, a_ref, dsk_ref, e_ref,
                         yp_ref, ea_ref, xw_ref, cd_ref):
    steps = x_ref.shape[0]
    e01 = e_ref[...]
    a_row = a_ref[0]
    dts = [dt_ref[0, t] for t in range(steps)]
    acum = []
    for t in range(steps):
        da = dts[t] * a_row
        acum.append(da if t == 0 else acum[-1] + da)
    xs = [x_ref[t].astype(F32) for t in range(steps)]
    bs = [b_ref[t].astype(F32) for t in range(steps)]
    cs = [c_ref[t].astype(F32) for t in range(steps)]
    for t in range(steps):
        acc = dsk_ref[0] * xs[t]
        for j in range(t + 1):
            gsc = jnp.sum(cs[t] * bs[j], axis=-1, keepdims=True)
            wgt = gsc * jnp.exp(acum[t] - acum[j]) * dts[j]
            acc = acc + _dot01_right(wgt, e01) * xs[j]
        yp_ref[t] = acc
        ea_ref[t] = _dot01_right(jnp.exp(acum[t]), e01)
        xw_ref[t] = (_dot01_right(dts[t] * jnp.exp(acum[steps - 1] - acum[t]), e01) * xs[t]).astype(xw_ref.dtype)
    cd_ref[0] = jnp.exp(acum[steps - 1])


def _ssd_sample_a(xbc3, dt, a_log, d_skip, groups, heads, hdim, n_state):
    steps, nb, _ = xbc3.shape
    rp = heads * hdim
    d_inner = groups * rp
    a = -jnp.exp(a_log.astype(F32)).reshape(groups, 1, heads)
    dtg = dt.reshape(steps, nb, groups, heads).transpose(2, 0, 1, 3)
    dsk = jnp.repeat(d_skip.astype(F32), hdim).reshape(groups, 1, rp)
    e01 = (jnp.arange(rp)[None, :] // hdim == jnp.arange(heads)[:, None]).astype(BF16)
    b_off = d_inner // n_state
    c_off = (d_inner + groups * n_state) // n_state
    big = pl.BlockSpec((steps, nb, rp), lambda g: (0, 0, g))
    return pl.pallas_call(
        _ssd_sample_a_kernel,
        grid=(groups,),
        in_specs=[big,
                  pl.BlockSpec((steps, nb, n_state), lambda g: (0, 0, b_off + g)),
                  pl.BlockSpec((steps, nb, n_state), lambda g: (0, 0, c_off + g)),
                  pl.BlockSpec((1, steps, nb, heads), lambda g: (g, 0, 0, 0)),
                  pl.BlockSpec((1, 1, heads), lambda g: (g, 0, 0)),
                  pl.BlockSpec((1, 1, rp), lambda g: (g, 0, 0)),
                  pl.BlockSpec((heads, rp), lambda g: (0, 0))],
        out_specs=[big, big, big, pl.BlockSpec((1, nb, heads), lambda g: (g, 0, 0))],
        out_shape=[jax.ShapeDtypeStruct((steps, nb, d_inner), F32),
                   jax.ShapeDtypeStruct((steps, nb, d_inner), F32),
                   jax.ShapeDtypeStruct((steps, nb, d_inner), BF16),
                   jax.ShapeDtypeStruct((groups, nb, heads), F32)],
        compiler_params=_cp("arbitrary"),
        name="ssd_sample_a",
    )(xbc3, xbc3, xbc3, dtg, a, dsk, e01)


def _ssd_sample_b_kernel(s_ref, c_ref, b_ref, xw_ref, ea_ref, yp_ref, cd_ref, y_ref, so_ref,
                         *, groups, heads, hdim):
    rp = heads * hdim
    n_state = s_ref.shape[2]
    rows8 = yp_ref.shape[1]
    cdl = jnp.broadcast_to(cd_ref[0], (groups * heads, n_state))
    for g in range(groups):
        s0 = s_ref[0, g * rp:(g + 1) * rp, :]
        cg = c_ref[0, :, g * n_state:(g + 1) * n_state]
        bg = b_ref[0, :, g * n_state:(g + 1) * n_state]
        yoff = lax.dot_general(cg, s0.astype(BF16), NT_DIMS, preferred_element_type=F32)
        cols = slice(g * rp, (g + 1) * rp)
        y_ref[0, :, cols] = yp_ref[0, :, cols] + ea_ref[0, :, cols] * yoff[:rows8]
        upd = lax.dot_general(xw_ref[0, :, cols], bg, TN_DIMS, preferred_element_type=F32)
        cdrows = jnp.concatenate(
            [jnp.broadcast_to(cdl[g * heads + r:g * heads + r + 1, :], (hdim, n_state))
             for r in range(heads)], axis=0)
        so_ref[0, g * rp:(g + 1) * rp, :] = cdrows * s0 + upd


def _ssd_sample_b(s0, c_b, b_b, xw_b, ea_b, yp_b, cd_col, groups, heads, hdim):
    nb, hp, n_state = s0.shape
    d_inner = hp
    r16 = c_b.shape[1]
    r8 = yp_b.shape[1]
    gn = c_b.shape[2]
    return pl.pallas_call(
        functools.partial(_ssd_sample_b_kernel, groups=groups, heads=heads, hdim=hdim),
        grid=(nb,),
        in_specs=[pl.BlockSpec((1, hp, n_state), lambda b: (b, 0, 0)),
                  pl.BlockSpec((1, r16, gn), lambda b: (b, 0, 0)),
                  pl.BlockSpec((1, r16, gn), lambda b: (b, 0, 0)),
                  pl.BlockSpec((1, r16, d_inner), lambda b: (b, 0, 0)),
                  pl.BlockSpec((1, r8, d_inner), lambda b: (b, 0, 0)),
                  pl.BlockSpec((1, r8, d_inner), lambda b: (b, 0, 0)),
                  pl.BlockSpec((1, groups * heads, 1), lambda b: (b, 0, 0))],
        out_specs=[pl.BlockSpec((1, r8, d_inner), lambda b: (b, 0, 0)),
                   pl.BlockSpec((1, hp, n_state), lambda b: (b, 0, 0))],
        out_shape=[jax.ShapeDtypeStruct((nb, r8, d_inner), F32),
                   jax.ShapeDtypeStruct((nb, hp, n_state), F32)],
        compiler_params=_cp("arbitrary"),
        name="ssd_sample_b",
    )(s0, c_b, b_b, xw_b, ea_b, yp_b, cd_col)


def _gatednorm_kernel(y_ref, z_ref, g_ref, o_ref):
    v = y_ref[...].astype(F32) * _silu(z_ref[...].astype(F32))
    o_ref[...] = (v * lax.rsqrt(jnp.mean(v * v, axis=-1, keepdims=True) + EPS) * g_ref[...]).astype(o_ref.dtype)


def _gatednorm(y, proj, z_off, g, tm):
    t, d = y.shape
    assert z_off % d == 0
    zo = z_off // d
    return pl.pallas_call(
        _gatednorm_kernel,
        grid=(t // tm,),
        in_specs=[pl.BlockSpec((tm, d), lambda i: (i, 0)),
                  pl.BlockSpec((tm, d), lambda i: (i, zo)),
                  pl.BlockSpec((1, d), lambda i: (0, 0))],
        out_specs=pl.BlockSpec((tm, d), lambda i: (i, 0)),
        out_shape=jax.ShapeDtypeStruct((t, d), BF16),
        compiler_params=_cp("arbitrary"),
        name="gatednorm",
    )(y, proj, g.reshape(1, d))


CONF_HALO = 32


def _layernorm_silu(v, g, b):
    mu = jnp.mean(v, axis=-1, keepdims=True)
    vc = v - mu
    var = jnp.mean(vc * vc, axis=-1, keepdims=True)
    return _silu(vc * lax.rsqrt(var + EPS) * g + b)


def _conf_prompt_kernel(ca_ref, cg_ref, hb_ref, w_ref, b_ref, g_ref, lb_ref, o_ref, tail_ref,
                        u_scr, v_scr, sh_scr, *, kc, lane_chunk):
    li = pl.program_id(1)
    tl = ca_ref.shape[1]
    d = ca_ref.shape[2]

    @pl.when(li == 0)
    def _():
        u_scr[0:CONF_HALO, :] = hb_ref[0]

    @pl.when(li > 0)
    def _():
        u_scr[0:CONF_HALO, :] = u_scr[tl:tl + CONF_HALO, :]

    u_scr[CONF_HALO:, :] = ca_ref[0].astype(F32) * jax.nn.sigmoid(cg_ref[0].astype(F32))

    span = tl + CONF_HALO - SUBLANES

    def chunk(ci, carry):
        lanes = pl.ds(pl.multiple_of(ci * lane_chunk, lane_chunk), lane_chunk)
        for r in range(1, SUBLANES):
            sh_scr[r - 1] = u_scr[r:r + span, lanes]
        acc = jnp.broadcast_to(b_ref[:, lanes], (tl, lane_chunk))
        for k in range(kc):
            a, r = divmod(CONF_HALO - (kc - 1) + k, SUBLANES)
            rows = pl.ds(a * SUBLANES, tl)
            src = u_scr[rows, lanes] if r == 0 else sh_scr[r - 1, rows, :]
            acc = acc + w_ref[k:k + 1, lanes] * src
        v_scr[:, lanes] = acc
        return carry

    lax.fori_loop(0, d // lane_chunk, chunk, 0)
    o_ref[...] = _layernorm_silu(v_scr[...], g_ref[...], lb_ref[...]).astype(o_ref.dtype)

    @pl.when(li == pl.num_programs(1) - 1)
    def _():
        tail_ref[0] = u_scr[tl:tl + CONF_HALO, :]


def _conf_prompt(proj3, hb, w, b, ln_g, ln_b, ca_off, cg_off):
    nb, seq, _ = proj3.shape
    kc, d = w.shape
    tl = _tile(seq, 128, CONF_HALO)
    nl = seq // tl
    assert ca_off % d == 0 and cg_off % d == 0 and tl >= CONF_HALO and kc - 1 <= CONF_HALO
    cao, cgo = ca_off // d, cg_off // d
    lane_chunk = _tile(d, 512)
    row = lambda bi, li: (0, 0)
    return pl.pallas_call(
        functools.partial(_conf_prompt_kernel, kc=kc, lane_chunk=lane_chunk),
        grid=(nb, nl),
        in_specs=[pl.BlockSpec((1, tl, d), lambda bi, li: (bi, li, cao)),
                  pl.BlockSpec((1, tl, d), lambda bi, li: (bi, li, cgo)),
                  pl.BlockSpec((1, CONF_HALO, d), lambda bi, li: (bi, 0, 0)),
                  pl.BlockSpec((kc, d), row),
                  pl.BlockSpec((1, d), row), pl.BlockSpec((1, d), row), pl.BlockSpec((1, d), row)],
        out_specs=[pl.BlockSpec((tl, d), lambda bi, li: (bi * nl + li, 0)),
                   pl.BlockSpec((1, CONF_HALO, d), lambda bi, li: (bi, 0, 0))],
        out_shape=[jax.ShapeDtypeStruct((nb * seq, d), BF16),
                   jax.ShapeDtypeStruct((nb, CONF_HALO, d), F32)],
        scratch_shapes=[pltpu.VMEM((tl + CONF_HALO, d), F32), pltpu.VMEM((tl, d), F32),
                        pltpu.VMEM((SUBLANES - 1, tl + CONF_HALO - SUBLANES, lane_chunk), F32)],
        compiler_params=_cp("arbitrary", "arbitrary"),
        name="conf_prompt",
    )(proj3, proj3, hb, w, b.reshape(1, d), ln_g.reshape(1, d), ln_b.reshape(1, d))


def _conf_sample_kernel(ca_ref, cg_ref, buf_ref, w_ref, b_ref, g_ref, lb_ref, o_ref, so_ref,
                        xp_scr, v_scr, *, kc):
    steps, bb, _ = ca_ref.shape
    win = w_ref.shape[0]
    nbuf = kc - 1
    ucs = [ca_ref[t].astype(F32) * jax.nn.sigmoid(cg_ref[t].astype(F32)) for t in range(steps)]
    xp_scr[nbuf + steps:, :] = jnp.zeros((xp_scr.shape[0] - nbuf - steps, xp_scr.shape[1]), F32)
    for b in range(bb):
        xp_scr[0:nbuf, :] = buf_ref[b]
        for t in range(steps):
            xp_scr[nbuf + t:nbuf + t + 1, :] = ucs[t][b:b + 1, :]
        for t in range(steps):
            v_scr[t, b:b + 1, :] = jnp.sum(w_ref[...] * xp_scr[t:t + win, :], axis=0, keepdims=True)
        so_ref[b] = xp_scr[steps:steps + nbuf, :]
    for t in range(steps):
        o_ref[t] = _layernorm_silu(v_scr[t] + b_ref[...], g_ref[...], lb_ref[...]).astype(o_ref.dtype)


def _conf_sample(proj3, buf, w, b, ln_g, ln_b, ca_off, cg_off):
    steps, nb, _ = proj3.shape
    kc, d = w.shape
    assert steps < kc - 1
    bb = SUBLANES
    win = -(-kc // SUBLANES) * SUBLANES
    rows = -(-(steps + win) // SUBLANES) * SUBLANES
    w_pad = jnp.pad(w, ((0, win - kc), (0, 0)))
    cao, cgo = ca_off // d, cg_off // d
    row = lambda i: (0, 0)
    return pl.pallas_call(
        functools.partial(_conf_sample_kernel, kc=kc),
        grid=(nb // bb,),
        in_specs=[pl.BlockSpec((steps, bb, d), lambda i: (0, i, cao)),
                  pl.BlockSpec((steps, bb, d), lambda i: (0, i, cgo)),
                  pl.BlockSpec((bb, kc - 1, d), lambda i: (i, 0, 0)),
                  pl.BlockSpec((win, d), row),
                  pl.BlockSpec((1, d), row), pl.BlockSpec((1, d), row), pl.BlockSpec((1, d), row)],
        out_specs=[pl.BlockSpec((steps, bb, d), lambda i: (0, i, 0)),
                   pl.BlockSpec((bb, kc - 1, d), lambda i: (i, 0, 0))],
        out_shape=[jax.ShapeDtypeStruct((steps, nb, d), BF16),
                   jax.ShapeDtypeStruct((nb, kc - 1, d), F32)],
        scratch_shapes=[pltpu.VMEM((rows, d), F32), pltpu.VMEM((steps, bb, d), F32)],
        compiler_params=_cp("arbitrary"),
        name="conf_sample",
    )(proj3, proj3, buf, w_pad, b.reshape(1, d), ln_g.reshape(1, d), ln_b.reshape(1, d))


def _top_values(s, k, scr):
    cur = s
    for r in range(k):
        m = jnp.max(cur, axis=0, keepdims=True)
        scr[r:r + 1, :] = m
        cur = jnp.where(cur >= m, NEG, cur)
    return scr[...]


def _candidate_sums(v1, v2, k):
    slabs = []
    a = 0
    while a < k and k // (a + 1) > 1:
        cnt = k // (a + 1)
        rows = -(-cnt // SUBLANES) * SUBLANES
        s = v1[a:a + 1, :] + v2[0:rows, :]
        if cnt < rows:
            s = jnp.where(lax.broadcasted_iota(jnp.int32, s.shape, 0) < cnt, s, NEG)
        slabs.append(s)
        a += 1
    if a < k:
        slabs.append(v1[a:k, :] + v2[0:1, :])
    return jnp.concatenate(slabs, axis=0)


def _peer_topk_kernel(q_ref, k1_ref, k2_ref, thr_ref, e1_ref, s2_ref, e2_ref, v1_scr, v2_scr,
                      *, n_heads, half, topk):
    for h in range(n_heads):
        base = h * 2 * half
        q1 = q_ref[:, base:base + half].astype(BF16)
        q2 = q_ref[:, base + half:base + 2 * half].astype(BF16)
        s1 = lax.dot_general(k1_ref[h].astype(BF16), q1, NT_DIMS, preferred_element_type=F32)
        s2 = lax.dot_general(k2_ref[h].astype(BF16), q2, NT_DIMS, preferred_element_type=F32)
        v1 = _top_values(s1, topk, v1_scr)
        v2 = _top_values(s2, topk, v2_scr)
        cand = _candidate_sums(v1, v2, topk)
        cur = cand
        tau = None
        for r in range(topk):
            tau = jnp.max(cur, axis=0, keepdims=True)
            if r < topk - 1:
                cur = jnp.where(cur >= tau, NEG, cur)
        top = v1[0:1, :] + v2[0:1, :]
        z = jnp.sum(jnp.where(cand >= tau, jnp.exp(cand - top), 0.0), axis=0, keepdims=True)
        thr_ref[h] = tau - s1
        e1_ref[h] = jnp.exp(s1 - v1[0:1, :]) / z
        s2_ref[h] = s2
        e2_ref[h] = jnp.exp(s2 - v2[0:1, :])


def _peer_topk(qv, keys1, keys2):
    t, _ = qv.shape
    n_heads, nk, half = keys1.shape
    tq = _tile(t, 256)
    kspec = pl.BlockSpec((n_heads, nk, half), lambda i: (0, 0, 0))
    ospec = pl.BlockSpec((n_heads, nk, tq), lambda i: (0, 0, i))
    oshape = jax.ShapeDtypeStruct((n_heads, nk, t), F32)
    return pl.pallas_call(
        functools.partial(_peer_topk_kernel, n_heads=n_heads, half=half, topk=PEER_TOPK),
        grid=(t // tq,),
        in_specs=[pl.BlockSpec((tq, n_heads * 2 * half), lambda i: (i, 0)), kspec, kspec],
        out_specs=[ospec] * 4,
        out_shape=[oshape] * 4,
        scratch_shapes=[pltpu.VMEM((PEER_TOPK, tq), F32), pltpu.VMEM((PEER_TOPK, tq), F32)],
        compiler_params=_cp("arbitrary"),
        name="peer_topk",
    )(qv, keys1, keys2)


def _gelu_exact(x):
    return 0.5 * x * (1.0 + lax.erf(x * 0.7071067811865476))


def _peer_dense_kernel(h_ref, u_ref, v_ref, thr_ref, e1_ref, s2_ref, e2_ref, o_ref, a_scr, g_scr, act_scr,
                       *, n_heads, nk):
    j = pl.program_id(1)
    n_blocks = pl.num_programs(1) - 1
    te = u_ref.shape[0]
    tm = h_ref.shape[0]
    d = o_ref.shape[1]
    dc = _tile(d, 512)
    tk = _tile(tm, 256)
    lt = _tile(tm, 128)

    def gate_piece(jb, a, c, key_rows):
        if (a, 0) not in key_rows:
            i1 = jb * (te // nk) + a
            for h in range(n_heads):
                key_rows[(a, h)] = (thr_ref[h, pl.ds(i1, 1), :], e1_ref[h, pl.ds(i1, 1), :])
        tok = slice(c * lt, (c + 1) * lt)
        w = jnp.zeros((nk, lt), F32)
        for h in range(n_heads):
            thr, e1 = key_rows[(a, h)]
            w = w + jnp.where(s2_ref[h, :, tok] >= thr[:, tok], e2_ref[h, :, tok] * e1[:, tok], 0.0)
        g_scr[a * nk:(a + 1) * nk, tok] = w

    def step(jb, slot, a_prev):
        pieces = [(a, c) for a in range(te // nk) for c in range(tm // lt)] if jb is not None else []
        n_mm = (tm // tk if jb is not None else 0) + (d // dc if a_prev is not None else 0)
        per = -(-len(pieces) // max(n_mm, 1))
        key_rows = {}

        def some_pieces():
            for _ in range(per):
                if pieces:
                    gate_piece(jb, *pieces.pop(0), key_rows)

        if jb is not None:
            for hf in range(tm // tk):
                tok = slice(hf * tk, (hf + 1) * tk)
                act_scr[:, tok] = _gelu_exact(lax.dot_general(u_ref[...], h_ref[tok, :], NT_DIMS,
                                                              preferred_element_type=F32))
                some_pieces()
        if a_prev is not None:
            for c in range(d // dc):
                cols = slice(c * dc, (c + 1) * dc)
                o_ref[:, cols] += jnp.dot(a_scr[a_prev], v_ref[:, cols], preferred_element_type=F32)
                some_pieces()
        if jb is not None:
            for c in range(tm // lt):
                tok = slice(c * lt, (c + 1) * lt)
                a_scr[slot, tok, :] = (act_scr[:, tok] * g_scr[:, tok]).T.astype(BF16)

    @pl.when(j == 0)
    def _():
        o_ref[...] = jnp.zeros_like(o_ref)
        step(0, 0, None)

    @pl.when(jnp.logical_and(j > 0, j < n_blocks))
    def _():
        slot = lax.rem(j, 2)
        step(j, slot, 1 - slot)

    @pl.when(j == n_blocks)
    def _():
        step(None, None, lax.rem(n_blocks - 1, 2))


def _peer_dense(h2, u, v, thr, e1, s2, e2):
    t, d = h2.shape
    n_exp = u.shape[0]
    n_heads, nk, _ = thr.shape
    tm = _tile(t, 512)
    te = 4 * nk
    n_blocks = n_exp // te
    once = pl.Buffered(1)
    fspec = pl.BlockSpec((n_heads, nk, tm), lambda i, j: (0, 0, i), pipeline_mode=once)
    return pl.pallas_call(
        functools.partial(_peer_dense_kernel, n_heads=n_heads, nk=nk),
        grid=(t // tm, n_blocks + 1),
        in_specs=[pl.BlockSpec((tm, d), lambda i, j: (i, 0), pipeline_mode=once),
                  pl.BlockSpec((te, d), lambda i, j: (jnp.minimum(j, n_blocks - 1), 0)),
                  pl.BlockSpec((te, d), lambda i, j: (jnp.maximum(j - 1, 0), 0)),
                  fspec, fspec, fspec, fspec],
        out_specs=pl.BlockSpec((tm, d), lambda i, j: (i, 0)),
        out_shape=jax.ShapeDtypeStruct((t, d), F32),
        scratch_shapes=[pltpu.VMEM((2, tm, te), BF16), pltpu.VMEM((te, tm), F32), pltpu.VMEM((te, tm), F32)],
        compiler_params=_cp("arbitrary", "arbitrary"),
        name="peer_dense",
    )(h2, u, v, thr, e1, s2, e2)


def _residual_kernel(x_ref, p_ref, gate_ref, g_ref, o_ref, *, final_norm):
    x = x_ref[...] + gate_ref[0] * p_ref[...]
    if final_norm:
        x = x * lax.rsqrt(jnp.mean(x * x, axis=-1, keepdims=True) + EPS) * g_ref[...]
    o_ref[...] = x


def _residual(x1, pe, grp, k_gate, g_final, final_norm):
    t, d = x1.shape
    tm = grp.row_tile(256)
    return pl.pallas_call(
        functools.partial(_residual_kernel, final_norm=final_norm),
        grid=(t // tm, 1),
        in_specs=[pl.BlockSpec((tm, d), lambda i, j: (i, 0)),
                  pl.BlockSpec((tm, d), lambda i, j: (i, 0)),
                  grp.mod_spec(tm, d),
                  pl.BlockSpec((1, d), lambda i, j: (0, 0))],
        out_specs=pl.BlockSpec((tm, d), lambda i, j: (i, 0)),
        out_shape=jax.ShapeDtypeStruct((t, d), F32),
        compiler_params=_cp("arbitrary", "arbitrary"),
        name="residual",
    )(x1, pe, grp.mods[k_gate], g_final.reshape(1, d))


def _softplus(x):
    return jnp.maximum(x, 0.0) + jnp.log1p(jnp.exp(-jnp.abs(x)))


def _layer_group(x, grp, prm, dims, ssd_fn, conv_fn, conf_fn, final_g, final_norm):
    d = dims["d"]
    d_inner = dims["d_inner"]
    off = dims["off"]
    tm = grp.row_tile(1024)
    tn = 1024

    h = _modnorm(x, prm["norm1_g"], grp, 1, 0)
    proj = _mm(h, prm["w_main"], tm=tm, tn=_tile(prm["w_main"].shape[1], tn), out_dtype=BF16, name="in_proj")
    hs = prm["w_dt"].shape[1]
    dt = _mm(h, prm["w_dt"], tm=tm, tn=hs, out_dtype=F32,
             epi=lambda acc, bias: _softplus(acc + bias),
             extras=[(prm["dt_bias"].reshape(1, hs), _row_spec(hs))], name="dt_proj")

    xbc = conv_fn(proj)
    y, ssm_new = ssd_fn(xbc, dt)
    yn = _gatednorm(y, proj, off["z"], prm["ssd_norm_g"], grp.row_tile(256))
    tn_d = _tile(d, 512)
    sa = _mm(yn, prm["w_ssd_out"], tm=grp.row_tile(512), tn=tn_d, out_dtype=F32,
             epi=lambda acc, ga: jax.nn.sigmoid(ga.astype(F32)) * acc,
             extras=[(proj, _tile_spec(grp.row_tile(512), tn_d, off["ga"]))], name="ssd_out")

    vc, conf_state = conf_fn(proj)
    tmm = grp.row_tile(512)
    tn_m = _tile(d, 1024)
    mixed = _mm(vc, prm["w_conf_out"], tm=tmm, tn=tn_m, out_dtype=BF16,
                epi=lambda acc, s, gb: s + jax.nn.sigmoid(gb.astype(F32)) * acc,
                extras=[(sa, _tile_spec(tmm, tn_m, 0)), (proj, _tile_spec(tmm, tn_m, off["gb"]))],
                name="conf_out")
    x1 = _mm(mixed, prm["w_out"], tm=tmm, tn=tn_m, out_dtype=F32,
             epi=lambda acc, xr, gate: xr + gate * acc,
             extras=[(x, _tile_spec(tmm, tn_m, 0)), (grp.mods[2], grp.mod_spec(tmm, tn_m))],
             name="out_proj")

    h2 = _modnorm(x1, prm["norm2_g"], grp, 4, 3)
    qv = _mm(h2, prm["peer_wq"], tm=tmm, tn=_tile(prm["peer_wq"].shape[1], 1024), out_dtype=F32, name="peer_q")
    thr, e1, s2, e2 = _peer_topk(qv, prm["peer_keys1"], prm["peer_keys2"])
    pe = _peer_dense(h2, prm["peer_u"], prm["peer_v"], thr, e1, s2, e2)
    x2 = _residual(x1, pe, grp, 5, final_g, final_norm)
    return x2, proj, ssm_new, conf_state


def kernel(x_prompt, x_sample, c_prompt, c_sample, state_ssm, state_ssd_conv, state_conf_conv, w_ada, b_ada, norm1_g, w_in, ssd_conv_w, ssd_conv_b, dt_bias, a_log, d_skip, ssd_norm_g, w_ssd_out, conf_dw_w, conf_dw_b, conf_ln_g, conf_ln_b, w_conf_out, w_out, norm2_g, peer_wq, peer_keys1, peer_keys2, peer_u, peer_v, final_norm_g):
    depth = w_ada.shape[0]
    bp, lp, d = x_prompt.shape
    bs, ls, _ = x_sample.shape
    n_state = state_ssm.shape[-1]
    hdim = state_ssm.shape[-2]
    n_ssm_heads = a_log.shape[-1]
    d_inner = ssd_norm_g.shape[-1]
    conv_dim = ssd_conv_w.shape[-1]
    groups = (conv_dim - d_inner) // (2 * n_state)
    heads = n_ssm_heads // groups
    d_conf = conf_dw_w.shape[-1]
    kc = conf_dw_w.shape[-2]
    ks = ssd_conv_w.shape[-2]
    n_mod = w_ada.shape[-1] // d

    c_z, c_xbc, c_dt = d_inner, d_inner + conv_dim, d_inner + conv_dim + n_ssm_heads
    off = {"z": 0, "ca": d_inner, "cg": d_inner + d_conf, "ga": d_inner + 2 * d_conf,
           "gb": d_inner + 2 * d_conf + d, "xbc": d_inner + 2 * d_conf + 2 * d}
    dims = {"d": d, "d_inner": d_inner, "off": off}

    xp = x_prompt.reshape(bp * lp, d)
    xs = x_sample.transpose(1, 0, 2).reshape(ls * bs, d)
    rows_c = bp + bs
    rows_pad = -(-rows_c // SUBLANES) * SUBLANES
    c_all = jnp.pad(jnp.concatenate([c_prompt, c_sample], axis=0), ((0, rows_pad - rows_c), (0, 0)))

    outs = {k: [] for k in ("ssm_p", "sconv_p", "cconv_p", "ssm_s", "sconv_s", "cconv_s")}
    for li in range(depth):
        final_norm = li == depth - 1
        mod = _adaln(c_all, w_ada[li], b_ada[li])
        mod_p = mod[:bp].reshape(bp, n_mod, d)
        mod_s = mod[bp:bp + bs].reshape(bs, n_mod, d)
        grp_p = _Group(bp * lp, lp, False, [mod_p[:, k][:, None, :] for k in range(n_mod)])
        grp_s = _Group(ls * bs, ls, True, [jnp.tile(mod_s[:, k], (ls, 1))[None] for k in range(n_mod)])

        wi = w_in[li]
        prm = {
            "norm1_g": norm1_g[li], "norm2_g": norm2_g[li], "ssd_norm_g": ssd_norm_g[li],
            "w_main": jnp.concatenate([wi[:, :c_z], wi[:, c_dt:], wi[:, c_z:c_xbc]], axis=1).astype(BF16),
            "w_dt": wi[:, c_xbc:c_dt].astype(BF16), "dt_bias": dt_bias[li].astype(F32),
            "w_ssd_out": w_ssd_out[li].astype(BF16), "w_conf_out": w_conf_out[li].astype(BF16),
            "w_out": w_out[li].astype(BF16), "peer_wq": peer_wq[li].astype(BF16),
            "peer_keys1": peer_keys1[li], "peer_keys2": peer_keys2[li],
            "peer_u": peer_u[li].astype(BF16), "peer_v": peer_v[li].astype(BF16),
        }
        ncols = prm["w_main"].shape[1]

        def conv_p(proj):
            hb = jnp.zeros((bp, SUBLANES, conv_dim), F32)
            return _ssdconv_prompt(proj.reshape(bp, lp, ncols), hb, ssd_conv_w[li], ssd_conv_b[li],
                                   off["xbc"], BF16).reshape(bp * lp, conv_dim)

        def ssd_p(xbc, dt):
            s0 = jnp.zeros((bp, n_ssm_heads * hdim, n_state), F32)
            return _ssd_prompt(xbc, dt, a_log[li], d_skip[li], s0, bp, lp, groups, heads, hdim, n_state)

        def conf_p(proj):
            hb = jnp.zeros((bp, CONF_HALO, d_conf), F32)
            return _conf_prompt(proj.reshape(bp, lp, ncols), hb, conf_dw_w[li], conf_dw_b[li],
                                conf_ln_g[li], conf_ln_b[li], off["ca"], off["cg"])

        xp, proj_p, ssm_p, tail_p = _layer_group(xp, grp_p, prm, dims, ssd_p, conv_p, conf_p,
                                                 final_norm_g, final_norm)
        outs["ssm_p"].append(ssm_p.reshape(bp, n_ssm_heads, hdim, n_state))
        keep_p = min(lp, ks - 1)
        xbc_tail_p = proj_p.reshape(bp, lp, ncols)[:, lp - keep_p:, off["xbc"]:].astype(F32)
        sconv0 = jnp.zeros((bp, ks - 1 - keep_p, conv_dim), F32)
        outs["sconv_p"].append(jnp.concatenate([sconv0, xbc_tail_p], axis=1))
        outs["cconv_p"].append(tail_p[:, CONF_HALO - (kc - 1):])

        sbuf = state_ssd_conv[li]
        cbuf = state_conf_conv[li]
        ssm0 = state_ssm[li].reshape(bs, n_ssm_heads * hdim, n_state)

        def conv_s(proj):
            return _ssdconv_sample(proj.reshape(ls, bs, ncols), sbuf.transpose(1, 0, 2), ssd_conv_w[li],
                                   ssd_conv_b[li], off["xbc"], BF16).reshape(ls * bs, conv_dim)

        def ssd_s(xbc, dt):
            xbc3 = xbc.reshape(ls, bs, conv_dim)
            yp, ea, xw, cd = _ssd_sample_a(xbc3, dt, a_log[li], d_skip[li], groups, heads, hdim, n_state)
            gn = groups * n_state

            def bmajor(v, rows):
                return jnp.pad(v.transpose(1, 0, 2), ((0, 0), (0, rows - ls), (0, 0)))

            r16 = -(-ls // 16) * 16
            r8 = -(-ls // SUBLANES) * SUBLANES
            b_b = bmajor(xbc3[:, :, d_inner:d_inner + gn], r16)
            c_b = bmajor(xbc3[:, :, d_inner + gn:], r16)
            cd_col = cd.transpose(1, 0, 2).reshape(bs, groups * heads, 1)
            y_b, s_new = _ssd_sample_b(ssm0, c_b, b_b, bmajor(xw, r16), bmajor(ea, r8), bmajor(yp, r8),
                                       cd_col, groups, heads, hdim)
            y = y_b[:, :ls].transpose(1, 0, 2).reshape(ls * bs, d_inner).astype(BF16)
            return y, s_new

        def conf_s(proj):
            vc, cnew = _conf_sample(proj.reshape(ls, bs, ncols), cbuf, conf_dw_w[li],
                                    conf_dw_b[li], conf_ln_g[li], conf_ln_b[li], off["ca"], off["cg"])
            return vc.reshape(ls * bs, d_conf), cnew

        xs, proj_s, ssm_s, cconv_s = _layer_group(xs, grp_s, prm, dims, ssd_s, conv_s, conf_s,
                                                  final_norm_g, final_norm)
        outs["ssm_s"].append(ssm_s.reshape(bs, n_ssm_heads, hdim, n_state))
        keep_s = min(ls, ks - 1)
        xbc_tail_s = proj_s.reshape(ls, bs, ncols)[ls - keep_s:, :, off["xbc"]:].astype(F32).transpose(1, 0, 2)
        outs["sconv_s"].append(jnp.concatenate([sbuf[:, keep_s:], xbc_tail_s], axis=1))
        outs["cconv_s"].append(cconv_s)

    y_prompt = xp.reshape(bp, lp, d)
    y_sample = xs.reshape(ls, bs, d).transpose(1, 0, 2)
    return (y_prompt, y_sample,
            jnp.stack(outs["ssm_p"], 0), jnp.stack(outs["sconv_p"], 0), jnp.stack(outs["cconv_p"], 0),
            jnp.stack(outs["ssm_s"], 0), jnp.stack(outs["sconv_s"], 0), jnp.stack(outs["cconv_s"], 0))
```

```python
import functools

import jax
import jax.numpy as jnp
from jax import lax
from jax.experimental import pallas as pl
from jax.experimental.pallas import tpu as pltpu

F32 = jnp.float32
BF16 = jnp.bfloat16
EPS = 1e-6
PEER_TOPK = 16
SSD_CHUNK = 128
NEG = -3.0e38
V7X_VMEM_LIMIT_BYTES = 56 * 1024 * 1024
SUBLANES = 8

NT_DIMS = (((1,), (1,)), ((), ()))
TN_DIMS = (((0,), (0,)), ((), ()))


def _cp(*sem):
    return pltpu.CompilerParams(dimension_semantics=sem,
                                vmem_limit_bytes=V7X_VMEM_LIMIT_BYTES)


def _tile(n, pref, mult=128):
    best = None
    t = mult
    while t <= min(n, pref):
        if n % t == 0:
            best = t
        t += mult
    return best if best is not None else n


def _silu(x):
    return x * jax.nn.sigmoid(x)


def _split3(a):
    hi = a.astype(BF16)
    r = a - hi.astype(F32)
    mid = r.astype(BF16)
    lo = (r - mid.astype(F32)).astype(BF16)
    return hi, mid, lo


def _dot01_left(m01, a):
    out = None
    for p in _split3(a):
        t = jnp.dot(m01, p, preferred_element_type=F32)
        out = t if out is None else out + t
    return out


def _dot01_right(a, m01):
    out = None
    for p in _split3(a):
        t = jnp.dot(p, m01, preferred_element_type=F32)
        out = t if out is None else out + t
    return out


class _Group:
    def __init__(self, n_tokens, seq_len, per_token, mods):
        self.T = n_tokens
        self.L = seq_len
        self.per_token = per_token
        self.mods = mods

    def mod_spec(self, tm, tn):
        if self.per_token:
            return pl.BlockSpec((1, tm, tn), lambda i, j: (0, i, j))
        tiles_per_batch = self.L // tm
        return pl.BlockSpec((1, 1, tn), lambda i, j: (i // tiles_per_batch, 0, j))

    def row_tile(self, pref):
        return _tile(self.T if self.per_token else self.L, pref, SUBLANES)


def _ada_kernel(c_ref, w_ref, b_ref, o_ref):
    a = _silu(c_ref[...]).astype(BF16)
    o_ref[...] = jnp.dot(a, w_ref[...].astype(BF16), preferred_element_type=F32) + b_ref[...]


def _adaln(c, w, b):
    m, d = c.shape
    n = w.shape[1]
    tn = _tile(n, 1024)
    return pl.pallas_call(
        _ada_kernel,
        grid=(n // tn,),
        in_specs=[pl.BlockSpec((m, d), lambda j: (0, 0)),
                  pl.BlockSpec((d, tn), lambda j: (0, j)),
                  pl.BlockSpec((1, tn), lambda j: (0, j))],
        out_specs=pl.BlockSpec((m, tn), lambda j: (0, j)),
        out_shape=jax.ShapeDtypeStruct((m, n), F32),
        compiler_params=_cp("arbitrary"),
        name="adaln",
    )(c, w, b.reshape(1, n))


def _modnorm_kernel(x_ref, g_ref, sc_ref, sh_ref, o_ref):
    x = x_ref[...]
    y = x * lax.rsqrt(jnp.mean(x * x, axis=-1, keepdims=True) + EPS) * g_ref[...]
    o_ref[...] = (y * (1.0 + sc_ref[0]) + sh_ref[0]).astype(o_ref.dtype)


def _modnorm(x, g, grp, k_scale, k_shift):
    t, d = x.shape
    tm = grp.row_tile(512)
    return pl.pallas_call(
        _modnorm_kernel,
        grid=(t // tm, 1),
        in_specs=[pl.BlockSpec((tm, d), lambda i, j: (i, 0)),
                  pl.BlockSpec((1, d), lambda i, j: (0, 0)),
                  grp.mod_spec(tm, d), grp.mod_spec(tm, d)],
        out_specs=pl.BlockSpec((tm, d), lambda i, j: (i, 0)),
        out_shape=jax.ShapeDtypeStruct((t, d), BF16),
        compiler_params=_cp("arbitrary", "arbitrary"),
        name="modnorm",
    )(x, g.reshape(1, d), grp.mods[k_scale], grp.mods[k_shift])


def _mm_kernel(a_ref, w_ref, *rest, epi, n_extra):
    extras = rest[:n_extra]
    o_ref = rest[n_extra]
    acc = jnp.dot(a_ref[...], w_ref[...], preferred_element_type=F32)
    vals = [e[0] if len(e.shape) == 3 else e[...] for e in extras]
    o_ref[...] = epi(acc, *vals).astype(o_ref.dtype)


def _mm(a, w, *, tm, tn, out_dtype, epi=None, extras=(), name="mm", w_cols=None):
    t, k = a.shape
    c0, n = (0, w.shape[1]) if w_cols is None else w_cols
    if epi is None:
        epi = lambda acc: acc
    lane = 128
    if c0 % tn != 0 and (c0 % lane != 0 or tn % lane != 0):
        w, c0 = w[:, c0:c0 + n], 0
    if c0 % tn == 0:
        w_spec = pl.BlockSpec((k, tn), lambda i, j: (0, j + c0 // tn))
    else:
        w_spec = pl.BlockSpec((pl.Element(k), pl.Element(tn)),
                              lambda i, j: (0, pl.multiple_of((c0 // lane + j * (tn // lane)) * lane, lane)))
    return pl.pallas_call(
        functools.partial(_mm_kernel, epi=epi, n_extra=len(extras)),
        grid=(t // tm, n // tn),
        in_specs=[pl.BlockSpec((tm, k), lambda i, j: (i, 0)), w_spec] + [s for _, s in extras],
        out_specs=pl.BlockSpec((tm, tn), lambda i, j: (i, j)),
        out_shape=jax.ShapeDtypeStruct((t, n), out_dtype),
        compiler_params=_cp("arbitrary", "arbitrary"),
        name=name,
    )(a, w, *[x for x, _ in extras])


def _tile_spec(tm, tn, col_off):
    assert col_off % tn == 0
    off = col_off // tn
    return pl.BlockSpec((tm, tn), lambda i, j: (i, j + off))


def _row_spec(tn):
    return pl.BlockSpec((1, tn), lambda i, j: (0, j))


def _ssdconv_prompt_kernel(x_ref, hb_ref, w_ref, b_ref, o_ref, u_scr, *, ks):
    seq = x_ref.shape[1]
    u_scr[0:SUBLANES, :] = hb_ref[0]
    u_scr[SUBLANES:, :] = x_ref[0].astype(F32)
    acc = b_ref[...]
    for k in range(ks):
        off = SUBLANES - (ks - 1) + k
        acc = acc + w_ref[k:k + 1, :] * u_scr[off:off + seq, :]
    o_ref[0] = _silu(acc).astype(o_ref.dtype)


def _ssdconv_prompt(proj3, hb, w, b, col_off, out_dtype):
    nb, seq, _ = proj3.shape
    ks, c = w.shape
    tc = _tile(c, 256)
    assert col_off % tc == 0
    off = col_off // tc
    return pl.pallas_call(
        functools.partial(_ssdconv_prompt_kernel, ks=ks),
        grid=(nb, c // tc),
        in_specs=[pl.BlockSpec((1, seq, tc), lambda bi, ci: (bi, 0, ci + off)),
                  pl.BlockSpec((1, SUBLANES, tc), lambda bi, ci: (bi, 0, ci)),
                  pl.BlockSpec((ks, tc), lambda bi, ci: (0, ci)),
                  pl.BlockSpec((1, tc), lambda bi, ci: (0, ci))],
        out_specs=pl.BlockSpec((1, seq, tc), lambda bi, ci: (bi, 0, ci)),
        out_shape=jax.ShapeDtypeStruct((nb, seq, c), out_dtype),
        scratch_shapes=[pltpu.VMEM((seq + SUBLANES, tc), F32)],
        compiler_params=_cp("arbitrary", "arbitrary"),
        name="ssdconv_prompt",
    )(proj3, hb, w, b.reshape(1, c))


def _ssdconv_sample_kernel(x_ref, buf_ref, w_ref, b_ref, o_ref, *, ks):
    steps = x_ref.shape[0]
    slabs = [buf_ref[m] for m in range(ks - 1)] + [x_ref[t].astype(F32) for t in range(steps)]
    for t in range(steps):
        acc = b_ref[...]
        for k in range(ks):
            acc = acc + w_ref[k:k + 1, :] * slabs[t + k]
        o_ref[t] = _silu(acc).astype(o_ref.dtype)


def _ssdconv_sample(proj3, buf_t, w, b, col_off, out_dtype):
    steps, nb, _ = proj3.shape
    ks, c = w.shape
    tc = _tile(c, 512)
    assert col_off % tc == 0
    off = col_off // tc
    return pl.pallas_call(
        functools.partial(_ssdconv_sample_kernel, ks=ks),
        grid=(c // tc,),
        in_specs=[pl.BlockSpec((steps, nb, tc), lambda ci: (0, 0, ci + off)),
                  pl.BlockSpec((ks - 1, nb, tc), lambda ci: (0, 0, ci)),
                  pl.BlockSpec((ks, tc), lambda ci: (0, ci)),
                  pl.BlockSpec((1, tc), lambda ci: (0, ci))],
        out_specs=pl.BlockSpec((steps, nb, tc), lambda ci: (0, 0, ci)),
        out_shape=jax.ShapeDtypeStruct((steps, nb, c), out_dtype),
        compiler_params=_cp("arbitrary"),
        name="ssdconv_sample",
    )(proj3, buf_t, w, b.reshape(1, c))


def _ssd_prompt_kernel(x_ref, b_ref, c_ref, dt_ref, dtT_ref, arow_ref, acol_ref, dsk_ref, s0_ref,
                       y_ref, sfin_ref, s_scr, *, heads, hdim, gb):
    ci = pl.program_id(2)
    q = x_ref.shape[0]
    n_state = b_ref.shape[1] // gb
    rp = heads * hdim

    @pl.when(ci == 0)
    def _():
        s_scr[...] = s0_ref[0]

    ii = lax.broadcasted_iota(jnp.int32, (q, q), 0)
    jj = lax.broadcasted_iota(jnp.int32, (q, q), 1)
    lower = jnp.where(ii >= jj, 1.0, 0.0).astype(BF16)
    upper = jnp.where(ii <= jj, 1.0, 0.0).astype(BF16)
    causal = ii <= jj

    def rows(v):
        return jnp.concatenate(
            [jnp.broadcast_to(v[r:r + 1, :], (hdim, v.shape[1])) for r in range(heads)], axis=0)

    for gi in range(gb):
        ch = slice(gi * rp, (gi + 1) * rp)
        st = slice(gi * n_state, (gi + 1) * n_state)
        dt = dt_ref[gi]
        dt_t = dtT_ref[gi]
        acum = _dot01_left(lower, dt * arow_ref[gi])
        acum_t = _dot01_right(dt_t * acol_ref[gi], upper)
        x_t = x_ref[:, ch].astype(F32).T
        bm = b_ref[:, st]
        cm = c_ref[:, st]
        scores_t = lax.dot_general(bm, cm, NT_DIMS, preferred_element_type=F32)
        last = acum_t[:, q - 1:q]
        xdt_t = (x_t * rows(dt_t)).astype(BF16)
        ydiag = []
        for r in range(heads):
            seg = acum_t[r:r + 1, :] - acum[:, r:r + 1]
            dec = jnp.exp(jnp.where(causal, seg, NEG))
            wgt = (scores_t * dec).astype(BF16)
            ydiag.append(jnp.dot(xdt_t[r * hdim:(r + 1) * hdim, :], wgt, preferred_element_type=F32))
        s_prev = s_scr[ch, :]
        yoff_t = lax.dot_general(s_prev.astype(BF16), cm, NT_DIMS, preferred_element_type=F32)
        y_t = jnp.concatenate(ydiag, axis=0) + yoff_t * rows(jnp.exp(acum_t)) + dsk_ref[gi] * x_t
        y_ref[:, ch] = y_t.T.astype(y_ref.dtype)

        coef_t = dt_t * jnp.exp(last - acum_t)
        chunk_s = jnp.dot((x_t * rows(coef_t)).astype(BF16), bm, preferred_element_type=F32)
        cd = jnp.broadcast_to(jnp.exp(last), (heads, n_state))
        s_scr[ch, :] = rows(cd) * s_prev + chunk_s

    @pl.when(ci == pl.num_programs(2) - 1)
    def _():
        sfin_ref[0] = s_scr[...]


def _ssd_prompt(xbc, dt, a_log, d_skip, s0, nb, seq, groups, heads, hdim, n_state):
    t = xbc.shape[0]
    q = SSD_CHUNK if seq % SSD_CHUNK == 0 else seq
    nc = seq // q
    rp = heads * hdim
    d_inner = groups * rp
    a = -jnp.exp(a_log.astype(F32))
    dtg = dt.reshape(t, groups, heads).transpose(1, 0, 2)
    dtg_t = dt.reshape(t, groups, heads).transpose(1, 2, 0)
    a_row = a.reshape(groups, 1, heads)
    a_col = a.reshape(groups, heads, 1)
    dsk = jnp.broadcast_to(jnp.repeat(d_skip.astype(F32), hdim).reshape(groups, rp, 1), (groups, rp, q))
    gb = next(g for g in (4, 2, 1) if groups % g == 0)
    ng = groups // gb
    gn = gb * n_state
    assert d_inner % gn == 0 and (groups * n_state) % gn == 0
    b_off = d_inner // gn
    c_off = (d_inner + groups * n_state) // gn
    return pl.pallas_call(
        functools.partial(_ssd_prompt_kernel, heads=heads, hdim=hdim, gb=gb),
        grid=(nb, ng, nc),
        in_specs=[pl.BlockSpec((q, gb * rp), lambda b, g, c: (b * nc + c, g)),
                  pl.BlockSpec((q, gn), lambda b, g, c: (b * nc + c, b_off + g)),
                  pl.BlockSpec((q, gn), lambda b, g, c: (b * nc + c, c_off + g)),
                  pl.BlockSpec((gb, q, heads), lambda b, g, c: (g, b * nc + c, 0)),
                  pl.BlockSpec((gb, heads, q), lambda b, g, c: (g, 0, b * nc + c)),
                  pl.BlockSpec((gb, 1, heads), lambda b, g, c: (g, 0, 0)),
                  pl.BlockSpec((gb, heads, 1), lambda b, g, c: (g, 0, 0)),
                  pl.BlockSpec((gb, rp, q), lambda b, g, c: (g, 0, 0)),
                  pl.BlockSpec((1, gb * rp, n_state), lambda b, g, c: (b, g, 0))],
        out_specs=[pl.BlockSpec((q, gb * rp), lambda b, g, c: (b * nc + c, g)),
                   pl.BlockSpec((1, gb * rp, n_state), lambda b, g, c: (b, g, 0))],
        out_shape=[jax.ShapeDtypeStruct((t, d_inner), BF16),
                   jax.ShapeDtypeStruct((nb, groups * rp, n_state), F32)],
        scratch_shapes=[pltpu.VMEM((gb * rp, n_state), F32)],
        compiler_params=_cp("arbitrary", "arbitrary", "arbitrary"),
        name="ssd_prompt",
    )(xbc, xbc, xbc, dtg, dtg_t, a_row, a_col, dsk, s0)


def _ssd_sample_a_kernel(x_ref, b_ref, c_ref, dt_ref, a_ref, dsk_ref, e_ref,
                         yp_ref, ea_ref, xw_ref, cd_ref):
    steps = x_ref.shape[0]
    e01 = e_ref[...]
    a_row = a_ref[0]
    dts = [dt_ref[0, t] for t in range(steps)]
    acum = []
    for t in range(steps):
        da = dts[t] * a_row
        acum.append(da if t == 0 else acum[-1] + da)
    xs = [x_ref[t].astype(F32) for t in range(steps)]
    bs = [b_ref[t].astype(F32) for t in range(steps)]
    cs = [c_ref[t].astype(F32) for t in range(steps)]
    for t in range(steps):
        acc = dsk_ref[0] * xs[t]
        for j in range(t + 1):
            gsc = jnp.sum(cs[t] * bs[j], axis=-1, keepdims=True)
            wgt = gsc * jnp.exp(acum[t] - acum[j]) * dts[j]
            acc = acc + _dot01_right(wgt, e01) * xs[j]
        yp_ref[t] = acc
        ea_ref[t] = _dot01_right(jnp.exp(acum[t]), e01)
        xw_ref[t] = (_dot01_right(dts[t] * jnp.exp(acum[steps - 1] - acum[t]), e01) * xs[t]).astype(xw_ref.dtype)
    cd_ref[0] = jnp.exp(acum[steps - 1])


def _ssd_sample_a(xbc3, dt, a_log, d_skip, groups, heads, hdim, n_state):
    steps, nb, _ = xbc3.shape
    rp = heads * hdim
    d_inner = groups * rp
    a = -jnp.exp(a_log.astype(F32)).reshape(groups, 1, heads)
    dtg = dt.reshape(steps, nb, groups, heads).transpose(2, 0, 1, 3)
    dsk = jnp.repeat(d_skip.astype(F32), hdim).reshape(groups, 1, rp)
    e01 = (jnp.arange(rp)[None, :] // hdim == jnp.arange(heads)[:, None]).astype(BF16)
    b_off = d_inner // n_state
    c_off = (d_inner + groups * n_state) // n_state
    big = pl.BlockSpec((steps, nb, rp), lambda g: (0, 0, g))
    return pl.pallas_call(
        _ssd_sample_a_kernel,
        grid=(groups,),
        in_specs=[big,
                  pl.BlockSpec((steps, nb, n_state), lambda g: (0, 0, b_off + g)),
                  pl.BlockSpec((steps, nb, n_state), lambda g: (0, 0, c_off + g)),
                  pl.BlockSpec((1, steps, nb, heads), lambda g: (g, 0, 0, 0)),
                  pl.BlockSpec((1, 1, heads), lambda g: (g, 0, 0)),
                  pl.BlockSpec((1, 1, rp), lambda g: (g, 0, 0)),
                  pl.BlockSpec((heads, rp), lambda g: (0, 0))],
        out_specs=[big, big, big, pl.BlockSpec((1, nb, heads), lambda g: (g, 0, 0))],
        out_shape=[jax.ShapeDtypeStruct((steps, nb, d_inner), F32),
                   jax.ShapeDtypeStruct((steps, nb, d_inner), F32),
                   jax.ShapeDtypeStruct((steps, nb, d_inner), BF16),
                   jax.ShapeDtypeStruct((groups, nb, heads), F32)],
        compiler_params=_cp("arbitrary"),
        name="ssd_sample_a",
    )(xbc3, xbc3, xbc3, dtg, a, dsk, e01)


def _ssd_sample_b_kernel(s_ref, c_ref, b_ref, xw_ref, ea_ref, yp_ref, cd_ref, y_ref, so_ref,
                         *, groups, heads, hdim):
    rp = heads * hdim
    n_state = s_ref.shape[2]
    rows8 = yp_ref.shape[1]
    cdl = jnp.broadcast_to(cd_ref[0], (groups * heads, n_state))
    for g in range(groups):
        s0 = s_ref[0, g * rp:(g + 1) * rp, :]
        cg = c_ref[0, :, g * n_state:(g + 1) * n_state]
        bg = b_ref[0, :, g * n_state:(g + 1) * n_state]
        yoff = lax.dot_general(cg, s0.astype(BF16), NT_DIMS, preferred_element_type=F32)
        cols = slice(g * rp, (g + 1) * rp)
        y_ref[0, :, cols] = yp_ref[0, :, cols] + ea_ref[0, :, cols] * yoff[:rows8]
        upd = lax.dot_general(xw_ref[0, :, cols], bg, TN_DIMS, preferred_element_type=F32)
        cdrows = jnp.concatenate(
            [jnp.broadcast_to(cdl[g * heads + r:g * heads + r + 1, :], (hdim, n_state))
             for r in range(heads)], axis=0)
        so_ref[0, g * rp:(g + 1) * rp, :] = cdrows * s0 + upd


def _ssd_sample_b(s0, c_b, b_b, xw_b, ea_b, yp_b, cd_col, groups, heads, hdim):
    nb, hp, n_state = s0.shape
    d_inner = hp
    r16 = c_b.shape[1]
    r8 = yp_b.shape[1]
    gn = c_b.shape[2]
    return pl.pallas_call(
        functools.partial(_ssd_sample_b_kernel, groups=groups, heads=heads, hdim=hdim),
        grid=(nb,),
        in_specs=[pl.BlockSpec((1, hp, n_state), lambda b: (b, 0, 0)),
                  pl.BlockSpec((1, r16, gn), lambda b: (b, 0, 0)),
                  pl.BlockSpec((1, r16, gn), lambda b: (b, 0, 0)),
                  pl.BlockSpec((1, r16, d_inner), lambda b: (b, 0, 0)),
                  pl.BlockSpec((1, r8, d_inner), lambda b: (b, 0, 0)),
                  pl.BlockSpec((1, r8, d_inner), lambda b: (b, 0, 0)),
                  pl.BlockSpec((1, groups * heads, 1), lambda b: (b, 0, 0))],
        out_specs=[pl.BlockSpec((1, r8, d_inner), lambda b: (b, 0, 0)),
                   pl.BlockSpec((1, hp, n_state), lambda b: (b, 0, 0))],
        out_shape=[jax.ShapeDtypeStruct((nb, r8, d_inner), F32),
                   jax.ShapeDtypeStruct((nb, hp, n_state), F32)],
        compiler_params=_cp("arbitrary"),
        name="ssd_sample_b",
    )(s0, c_b, b_b, xw_b, ea_b, yp_b, cd_col)


def _gatednorm_kernel(y_ref, z_ref, g_ref, o_ref):
    v = y_ref[...].astype(F32) * _silu(z_ref[...].astype(F32))
    o_ref[...] = (v * lax.rsqrt(jnp.mean(v * v, axis=-1, keepdims=True) + EPS) * g_ref[...]).astype(o_ref.dtype)


def _gatednorm(y, proj, z_off, g, tm):
    t, d = y.shape
    assert z_off % d == 0
    zo = z_off // d
    return pl.pallas_call(
        _gatednorm_kernel,
        grid=(t // tm,),
        in_specs=[pl.BlockSpec((tm, d), lambda i: (i, 0)),
                  pl.BlockSpec((tm, d), lambda i: (i, zo)),
                  pl.BlockSpec((1, d), lambda i: (0, 0))],
        out_specs=pl.BlockSpec((tm, d), lambda i: (i, 0)),
        out_shape=jax.ShapeDtypeStruct((t, d), BF16),
        compiler_params=_cp("arbitrary"),
        name="gatednorm",
    )(y, proj, g.reshape(1, d))


CONF_HALO = 32


def _layernorm_silu(v, g, b):
    mu = jnp.mean(v, axis=-1, keepdims=True)
    vc = v - mu
    var = jnp.mean(vc * vc, axis=-1, keepdims=True)
    return _silu(vc * lax.rsqrt(var + EPS) * g + b)


def _conf_prompt_kernel(ca_ref, cg_ref, hb_ref, w_ref, b_ref, g_ref, lb_ref, o_ref, tail_ref,
                        u_scr, v_scr, sh_scr, *, kc, lane_chunk):
    li = pl.program_id(1)
    tl = ca_ref.shape[1]
    d = ca_ref.shape[2]

    @pl.when(li == 0)
    def _():
        u_scr[0:CONF_HALO, :] = hb_ref[0]

    @pl.when(li > 0)
    def _():
        u_scr[0:CONF_HALO, :] = u_scr[tl:tl + CONF_HALO, :]

    u_scr[CONF_HALO:, :] = ca_ref[0].astype(F32) * jax.nn.sigmoid(cg_ref[0].astype(F32))

    span = tl + CONF_HALO - SUBLANES

    def chunk(ci, carry):
        lanes = pl.ds(pl.multiple_of(ci * lane_chunk, lane_chunk), lane_chunk)
        for r in range(1, SUBLANES):
            sh_scr[r - 1] = u_scr[r:r + span, lanes]
        acc = jnp.broadcast_to(b_ref[:, lanes], (tl, lane_chunk))
        for k in range(kc):
            a, r = divmod(CONF_HALO - (kc - 1) + k, SUBLANES)
            rows = pl.ds(a * SUBLANES, tl)
            src = u_scr[rows, lanes] if r == 0 else sh_scr[r - 1, rows, :]
            acc = acc + w_ref[k:k + 1, lanes] * src
        v_scr[:, lanes] = acc
        return carry

    lax.fori_loop(0, d // lane_chunk, chunk, 0)
    o_ref[...] = _layernorm_silu(v_scr[...], g_ref[...], lb_ref[...]).astype(o_ref.dtype)

    @pl.when(li == pl.num_programs(1) - 1)
    def _():
        tail_ref[0] = u_scr[tl:tl + CONF_HALO, :]


def _conf_prompt(proj3, hb, w, b, ln_g, ln_b, ca_off, cg_off):
    nb, seq, _ = proj3.shape
    kc, d = w.shape
    tl = _tile(seq, 128, CONF_HALO)
    nl = seq // tl
    assert ca_off % d == 0 and cg_off % d == 0 and tl >= CONF_HALO and kc - 1 <= CONF_HALO
    cao, cgo = ca_off // d, cg_off // d
    lane_chunk = _tile(d, 512)
    row = lambda bi, li: (0, 0)
    return pl.pallas_call(
        functools.partial(_conf_prompt_kernel, kc=kc, lane_chunk=lane_chunk),
        grid=(nb, nl),
        in_specs=[pl.BlockSpec((1, tl, d), lambda bi, li: (bi, li, cao)),
                  pl.BlockSpec((1, tl, d), lambda bi, li: (bi, li, cgo)),
                  pl.BlockSpec((1, CONF_HALO, d), lambda bi, li: (bi, 0, 0)),
                  pl.BlockSpec((kc, d), row),
                  pl.BlockSpec((1, d), row), pl.BlockSpec((1, d), row), pl.BlockSpec((1, d), row)],
        out_specs=[pl.BlockSpec((tl, d), lambda bi, li: (bi * nl + li, 0)),
                   pl.BlockSpec((1, CONF_HALO, d), lambda bi, li: (bi, 0, 0))],
        out_shape=[jax.ShapeDtypeStruct((nb * seq, d), BF16),
                   jax.ShapeDtypeStruct((nb, CONF_HALO, d), F32)],
        scratch_shapes=[pltpu.VMEM((tl + CONF_HALO, d), F32), pltpu.VMEM((tl, d), F32),
                        pltpu.VMEM((SUBLANES - 1, tl + CONF_HALO - SUBLANES, lane_chunk), F32)],
        compiler_params=_cp("arbitrary", "arbitrary"),
        name="conf_prompt",
    )(proj3, proj3, hb, w, b.reshape(1, d), ln_g.reshape(1, d), ln_b.reshape(1, d))


def _conf_sample_kernel(ca_ref, cg_ref, buf_ref, w_ref, b_ref, g_ref, lb_ref, o_ref, so_ref,
                        xp_scr, v_scr, *, kc):
    steps, bb, _ = ca_ref.shape
    win = w_ref.shape[0]
    nbuf = kc - 1
    ucs = [ca_ref[t].astype(F32) * jax.nn.sigmoid(cg_ref[t].astype(F32)) for t in range(steps)]
    xp_scr[nbuf + steps:, :] = jnp.zeros((xp_scr.shape[0] - nbuf - steps, xp_scr.shape[1]), F32)
    for b in range(bb):
        xp_scr[0:nbuf, :] = buf_ref[b]
        for t in range(steps):
            xp_scr[nbuf + t:nbuf + t + 1, :] = ucs[t][b:b + 1, :]
        for t in range(steps):
            v_scr[t, b:b + 1, :] = jnp.sum(w_ref[...] * xp_scr[t:t + win, :], axis=0, keepdims=True)
        so_ref[b] = xp_scr[steps:steps + nbuf, :]
    for t in range(steps):
        o_ref[t] = _layernorm_silu(v_scr[t] + b_ref[...], g_ref[...], lb_ref[...]).astype(o_ref.dtype)


def _conf_sample(proj3, buf, w, b, ln_g, ln_b, ca_off, cg_off):
    steps, nb, _ = proj3.shape
    kc, d = w.shape
    assert steps < kc - 1
    bb = SUBLANES
    win = -(-kc // SUBLANES) * SUBLANES
    rows = -(-(steps + win) // SUBLANES) * SUBLANES
    w_pad = jnp.pad(w, ((0, win - kc), (0, 0)))
    cao, cgo = ca_off // d, cg_off // d
    row = lambda i: (0, 0)
    return pl.pallas_call(
        functools.partial(_conf_sample_kernel, kc=kc),
        grid=(nb // bb,),
        in_specs=[pl.BlockSpec((steps, bb, d), lambda i: (0, i, cao)),
                  pl.BlockSpec((steps, bb, d), lambda i: (0, i, cgo)),
                  pl.BlockSpec((bb, kc - 1, d), lambda i: (i, 0, 0)),
                  pl.BlockSpec((win, d), row),
                  pl.BlockSpec((1, d), row), pl.BlockSpec((1, d), row), pl.BlockSpec((1, d), row)],
        out_specs=[pl.BlockSpec((steps, bb, d), lambda i: (0, i, 0)),
                   pl.BlockSpec((bb, kc - 1, d), lambda i: (i, 0, 0))],
        out_shape=[jax.ShapeDtypeStruct((steps, nb, d), BF16),
                   jax.ShapeDtypeStruct((nb, kc - 1, d), F32)],
        scratch_shapes=[pltpu.VMEM((rows, d), F32), pltpu.VMEM((steps, bb, d), F32)],
        compiler_params=_cp("arbitrary"),
        name="conf_sample",
    )(proj3, proj3, buf, w_pad, b.reshape(1, d), ln_g.reshape(1, d), ln_b.reshape(1, d))


def _top_values(s, k, scr):
    cur = s
    for r in range(k):
        m = jnp.max(cur, axis=0, keepdims=True)
        scr[r:r + 1, :] = m
        cur = jnp.where(cur >= m, NEG, cur)
    return scr[...]


def _candidate_sums(v1, v2, k):
    slabs = []
    a = 0
    while a < k and k // (a + 1) > 1:
        cnt = k // (a + 1)
        rows = -(-cnt // SUBLANES) * SUBLANES
        s = v1[a:a + 1, :] + v2[0:rows, :]
        if cnt < rows:
            s = jnp.where(lax.broadcasted_iota(jnp.int32, s.shape, 0) < cnt, s, NEG)
        slabs.append(s)
        a += 1
    if a < k:
        slabs.append(v1[a:k, :] + v2[0:1, :])
    return jnp.concatenate(slabs, axis=0)


def _peer_topk_kernel(q_ref, k1_ref, k2_ref, thr_ref, e1_ref, s2_ref, e2_ref, v1_scr, v2_scr,
                      *, n_heads, half, topk):
    for h in range(n_heads):
        base = h * 2 * half
        q1 = q_ref[:, base:base + half].astype(BF16)
        q2 = q_ref[:, base + half:base + 2 * half].astype(BF16)
        s1 = lax.dot_general(k1_ref[h].astype(BF16), q1, NT_DIMS, preferred_element_type=F32)
        s2 = lax.dot_general(k2_ref[h].astype(BF16), q2, NT_DIMS, preferred_element_type=F32)
        v1 = _top_values(s1, topk, v1_scr)
        v2 = _top_values(s2, topk, v2_scr)
        cand = _candidate_sums(v1, v2, topk)
        cur = cand
        tau = None
        for r in range(topk):
            tau = jnp.max(cur, axis=0, keepdims=True)
            if r < topk - 1:
                cur = jnp.where(cur >= tau, NEG, cur)
        top = v1[0:1, :] + v2[0:1, :]
        z = jnp.sum(jnp.where(cand >= tau, jnp.exp(cand - top), 0.0), axis=0, keepdims=True)
        thr_ref[h] = tau - s1
        e1_ref[h] = jnp.exp(s1 - v1[0:1, :]) / z
        s2_ref[h] = s2
        e2_ref[h] = jnp.exp(s2 - v2[0:1, :])


def _peer_topk(qv, keys1, keys2):
    t, _ = qv.shape
    n_heads, nk, half = keys1.shape
    tq = _tile(t, 256)
    kspec = pl.BlockSpec((n_heads, nk, half), lambda i: (0, 0, 0))
    ospec = pl.BlockSpec((n_heads, nk, tq), lambda i: (0, 0, i))
    oshape = jax.ShapeDtypeStruct((n_heads, nk, t), F32)
    return pl.pallas_call(
        functools.partial(_peer_topk_kernel, n_heads=n_heads, half=half, topk=PEER_TOPK),
        grid=(t // tq,),
        in_specs=[pl.BlockSpec((tq, n_heads * 2 * half), lambda i: (i, 0)), kspec, kspec],
        out_specs=[ospec] * 4,
        out_shape=[oshape] * 4,
        scratch_shapes=[pltpu.VMEM((PEER_TOPK, tq), F32), pltpu.VMEM((PEER_TOPK, tq), F32)],
        compiler_params=_cp("arbitrary"),
        name="peer_topk",
    )(qv, keys1, keys2)


def _gelu_exact(x):
    return 0.5 * x * (1.0 + lax.erf(x * 0.7071067811865476))


def _peer_dense_kernel(h_ref, u_ref, v_ref, thr_ref, e1_ref, s2_ref, e2_ref, o_ref, a_scr, g_scr, act_scr,
                       *, n_heads, nk):
    j = pl.program_id(1)
    n_blocks = pl.num_programs(1) - 1
    te = u_ref.shape[0]
    tm = h_ref.shape[0]
    d = o_ref.shape[1]
    dc = _tile(d, 512)
    tk = _tile(tm, 256)
    lt = _tile(tm, 128)

    def gate_piece(jb, a, c, key_rows):
        if (a, 0) not in key_rows:
            i1 = jb * (te // nk) + a
            for h in range(n_heads):
                key_rows[(a, h)] = (thr_ref[h, pl.ds(i1, 1), :], e1_ref[h, pl.ds(i1, 1), :])
        tok = slice(c * lt, (c + 1) * lt)
        w = jnp.zeros((nk, lt), F32)
        for h in range(n_heads):
            thr, e1 = key_rows[(a, h)]
            w = w + jnp.where(s2_ref[h, :, tok] >= thr[:, tok], e2_ref[h, :, tok] * e1[:, tok], 0.0)
        g_scr[a * nk:(a + 1) * nk, tok] = w

    def step(jb, slot, a_prev):
        pieces = [(a, c) for a in range(te // nk) for c in range(tm // lt)] if jb is not None else []
        n_mm = (tm // tk if jb is not None else 0) + (d // dc if a_prev is not None else 0)
        per = -(-len(pieces) // max(n_mm, 1))
        key_rows = {}

        def some_pieces():
            for _ in range(per):
                if pieces:
                    gate_piece(jb, *pieces.pop(0), key_rows)

        if jb is not None:
            for hf in range(tm // tk):
                tok = slice(hf * tk, (hf + 1) * tk)
                act_scr[:, tok] = _gelu_exact(lax.dot_general(u_ref[...], h_ref[tok, :], NT_DIMS,
                                                              preferred_element_type=F32))
                some_pieces()
        if a_prev is not None:
            for c in range(d // dc):
                cols = slice(c * dc, (c + 1) * dc)
                o_ref[:, cols] += jnp.dot(a_scr[a_prev], v_ref[:, cols], preferred_element_type=F32)
                some_pieces()
        if jb is not None:
            for c in range(tm // lt):
                tok = slice(c * lt, (c + 1) * lt)
                a_scr[slot, tok, :] = (act_scr[:, tok] * g_scr[:, tok]).T.astype(BF16)

    @pl.when(j == 0)
    def _():
        o_ref[...] = jnp.zeros_like(o_ref)
        step(0, 0, None)

    @pl.when(jnp.logical_and(j > 0, j < n_blocks))
    def _():
        slot = lax.rem(j, 2)
        step(j, slot, 1 - slot)

    @pl.when(j == n_blocks)
    def _():
        step(None, None, lax.rem(n_blocks - 1, 2))


def _peer_dense(h2, u, v, thr, e1, s2, e2):
    t, d = h2.shape
    n_exp = u.shape[0]
    n_heads, nk, _ = thr.shape
    tm = _tile(t, 512)
    te = 4 * nk
    n_blocks = n_exp // te
    once = pl.Buffered(1)
    fspec = pl.BlockSpec((n_heads, nk, tm), lambda i, j: (0, 0, i), pipeline_mode=once)
    return pl.pallas_call(
        functools.partial(_peer_dense_kernel, n_heads=n_heads, nk=nk),
        grid=(t // tm, n_blocks + 1),
        in_specs=[pl.BlockSpec((tm, d), lambda i, j: (i, 0), pipeline_mode=once),
                  pl.BlockSpec((te, d), lambda i, j: (jnp.minimum(j, n_blocks - 1), 0)),
                  pl.BlockSpec((te, d), lambda i, j: (jnp.maximum(j - 1, 0), 0)),
                  fspec, fspec, fspec, fspec],
        out_specs=pl.BlockSpec((tm, d), lambda i, j: (i, 0)),
        out_shape=jax.ShapeDtypeStruct((t, d), F32),
        scratch_shapes=[pltpu.VMEM((2, tm, te), BF16), pltpu.VMEM((te, tm), F32), pltpu.VMEM((te, tm), F32)],
        compiler_params=_cp("arbitrary", "arbitrary"),
        name="peer_dense",
    )(h2, u, v, thr, e1, s2, e2)


def _residual_kernel(x_ref, p_ref, gate_ref, g_ref, o_ref, *, final_norm):
    x = x_ref[...] + gate_ref[0] * p_ref[...]
    if final_norm:
        x = x * lax.rsqrt(jnp.mean(x * x, axis=-1, keepdims=True) + EPS) * g_ref[...]
    o_ref[...] = x


def _residual(x1, pe, grp, k_gate, g_final, final_norm):
    t, d = x1.shape
    tm = grp.row_tile(256)
    return pl.pallas_call(
        functools.partial(_residual_kernel, final_norm=final_norm),
        grid=(t // tm, 1),
        in_specs=[pl.BlockSpec((tm, d), lambda i, j: (i, 0)),
                  pl.BlockSpec((tm, d), lambda i, j: (i, 0)),
                  grp.mod_spec(tm, d),
                  pl.BlockSpec((1, d), lambda i, j: (0, 0))],
        out_specs=pl.BlockSpec((tm, d), lambda i, j: (i, 0)),
        out_shape=jax.ShapeDtypeStruct((t, d), F32),
        compiler_params=_cp("arbitrary", "arbitrary"),
        name="residual",
    )(x1, pe, grp.mods[k_gate], g_final.reshape(1, d))


def _softplus(x):
    return jnp.maximum(x, 0.0) + jnp.log1p(jnp.exp(-jnp.abs(x)))


def _layer_group(x, grp, prm, dims, ssd_fn, conv_fn, conf_fn, final_g, final_norm):
    d = dims["d"]
    d_inner = dims["d_inner"]
    off = dims["off"]
    tm = grp.row_tile(1024)
    tn = 1024

    h = _modnorm(x, prm["norm1_g"], grp, 1, 0)
    (a0, na), (b0, nb) = dims["cols_a"], dims["cols_b"]
    proj_a = _mm(h, prm["w_in"], tm=tm, tn=_tile(na, tn), out_dtype=BF16, name="in_proj_a", w_cols=(a0, na))
    proj_b = _mm(h, prm["w_in"], tm=tm, tn=_tile(nb, tn), out_dtype=BF16, name="in_proj_b", w_cols=(b0, nb))
    hs = prm["w_dt"].shape[1]
    dt = _mm(h, prm["w_dt"], tm=tm, tn=hs, out_dtype=F32,
             epi=lambda acc, bias: _softplus(acc + bias),
             extras=[(prm["dt_bias"].reshape(1, hs), _row_spec(hs))], name="dt_proj")

    xbc = conv_fn(proj_a)
    y, ssm_new = ssd_fn(xbc, dt)
    yn = _gatednorm(y, proj_a, off["z"], prm["ssd_norm_g"], grp.row_tile(256))
    tn_d = _tile(d, 512)
    sa = _mm(yn, prm["w_ssd_out"], tm=grp.row_tile(512), tn=tn_d, out_dtype=BF16,
             epi=lambda acc, ga: jax.nn.sigmoid(ga.astype(F32)) * acc,
             extras=[(proj_b, _tile_spec(grp.row_tile(512), tn_d, off["ga"]))], name="ssd_out")

    vc, conf_state = conf_fn(proj_b)
    tmm = grp.row_tile(512)
    tn_m = _tile(d, 1024)
    mixed = _mm(vc, prm["w_conf_out"], tm=tmm, tn=tn_m, out_dtype=BF16,
                epi=lambda acc, s, gb: s.astype(F32) + jax.nn.sigmoid(gb.astype(F32)) * acc,
                extras=[(sa, _tile_spec(tmm, tn_m, 0)), (proj_b, _tile_spec(tmm, tn_m, off["gb"]))],
                name="conf_out")
    x1 = _mm(mixed, prm["w_out"], tm=tmm, tn=tn_m, out_dtype=F32,
             epi=lambda acc, xr, gate: xr + gate * acc,
             extras=[(x, _tile_spec(tmm, tn_m, 0)), (grp.mods[2], grp.mod_spec(tmm, tn_m))],
             name="out_proj")

    h2 = _modnorm(x1, prm["norm2_g"], grp, 4, 3)
    qv = _mm(h2, prm["peer_wq"], tm=tmm, tn=_tile(prm["peer_wq"].shape[1], 1024), out_dtype=F32, name="peer_q")
    thr, e1, s2, e2 = _peer_topk(qv, prm["peer_keys1"], prm["peer_keys2"])
    pe = _peer_dense(h2, prm["peer_u"], prm["peer_v"], thr, e1, s2, e2)
    x2 = _residual(x1, pe, grp, 5, final_g, final_norm)
    return x2, proj_a, ssm_new, conf_state


def kernel(x_prompt, x_sample, c_prompt, c_sample, state_ssm, state_ssd_conv, state_conf_conv, w_ada, b_ada, norm1_g, w_in, ssd_conv_w, ssd_conv_b, dt_bias, a_log, d_skip, ssd_norm_g, w_ssd_out, conf_dw_w, conf_dw_b, conf_ln_g, conf_ln_b, w_conf_out, w_out, norm2_g, peer_wq, peer_keys1, peer_keys2, peer_u, peer_v, final_norm_g):
    depth = w_ada.shape[0]
    bp, lp, d = x_prompt.shape
    bs, ls, _ = x_sample.shape
    n_state = state_ssm.shape[-1]
    hdim = state_ssm.shape[-2]
    n_ssm_heads = a_log.shape[-1]
    d_inner = ssd_norm_g.shape[-1]
    conv_dim = ssd_conv_w.shape[-1]
    groups = (conv_dim - d_inner) // (2 * n_state)
    heads = n_ssm_heads // groups
    d_conf = conf_dw_w.shape[-1]
    kc = conf_dw_w.shape[-2]
    ks = ssd_conv_w.shape[-2]
    n_mod = w_ada.shape[-1] // d

    c_xbc, c_dt = d_inner + conv_dim, d_inner + conv_dim + n_ssm_heads
    off = {"z": 0, "xbc": d_inner,
           "ca": 0, "cg": d_conf, "ga": 2 * d_conf, "gb": 2 * d_conf + d}
    ncols_a, ncols_b = c_xbc, 2 * d_conf + 2 * d
    dims = {"d": d, "d_inner": d_inner, "off": off, "cols_a": (0, ncols_a), "cols_b": (c_dt, ncols_b)}

    xp = x_prompt.reshape(bp * lp, d)
    xs = x_sample.transpose(1, 0, 2).reshape(ls * bs, d)
    rows_c = bp + bs
    rows_pad = -(-rows_c // SUBLANES) * SUBLANES
    c_all = jnp.pad(jnp.concatenate([c_prompt, c_sample], axis=0), ((0, rows_pad - rows_c), (0, 0)))

    outs = {k: [] for k in ("ssm_p", "sconv_p", "cconv_p", "ssm_s", "sconv_s", "cconv_s")}
    for li in range(depth):
        final_norm = li == depth - 1
        mod = _adaln(c_all, w_ada[li], b_ada[li])
        mod_p = mod[:bp].reshape(bp, n_mod, d)
        mod_s = mod[bp:bp + bs].reshape(bs, n_mod, d)
        grp_p = _Group(bp * lp, lp, False, [mod_p[:, k][:, None, :] for k in range(n_mod)])
        grp_s = _Group(ls * bs, ls, True, [jnp.tile(mod_s[:, k], (ls, 1))[None] for k in range(n_mod)])

        wi = w_in[li].astype(BF16)
        prm = {
            "norm1_g": norm1_g[li], "norm2_g": norm2_g[li], "ssd_norm_g": ssd_norm_g[li],
            "w_in": wi, "w_dt": wi[:, c_xbc:c_dt], "dt_bias": dt_bias[li].astype(F32),
            "w_ssd_out": w_ssd_out[li].astype(BF16), "w_conf_out": w_conf_out[li].astype(BF16),
            "w_out": w_out[li].astype(BF16), "peer_wq": peer_wq[li].astype(BF16),
            "peer_keys1": peer_keys1[li], "peer_keys2": peer_keys2[li],
            "peer_u": peer_u[li].astype(BF16), "peer_v": peer_v[li].astype(BF16),
        }

        def conv_p(proj):
            hb = jnp.zeros((bp, SUBLANES, conv_dim), F32)
            return _ssdconv_prompt(proj.reshape(bp, lp, ncols_a), hb, ssd_conv_w[li], ssd_conv_b[li],
                                   off["xbc"], BF16).reshape(bp * lp, conv_dim)

        def ssd_p(xbc, dt):
            s0 = jnp.zeros((bp, n_ssm_heads * hdim, n_state), F32)
            return _ssd_prompt(xbc, dt, a_log[li], d_skip[li], s0, bp, lp, groups, heads, hdim, n_state)

        def conf_p(proj):
            hb = jnp.zeros((bp, CONF_HALO, d_conf), F32)
            return _conf_prompt(proj.reshape(bp, lp, ncols_b), hb, conf_dw_w[li], conf_dw_b[li],
                                conf_ln_g[li], conf_ln_b[li], off["ca"], off["cg"])

        xp, proj_p, ssm_p, tail_p = _layer_group(xp, grp_p, prm, dims, ssd_p, conv_p, conf_p,
                                                 final_norm_g, final_norm)
        outs["ssm_p"].append(ssm_p.reshape(bp, n_ssm_heads, hdim, n_state))
        keep_p = min(lp, ks - 1)
        xbc_tail_p = proj_p.reshape(bp, lp, ncols_a)[:, lp - keep_p:, off["xbc"]:].astype(F32)
        sconv0 = jnp.zeros((bp, ks - 1 - keep_p, conv_dim), F32)
        outs["sconv_p"].append(jnp.concatenate([sconv0, xbc_tail_p], axis=1))
        outs["cconv_p"].append(tail_p[:, CONF_HALO - (kc - 1):])

        sbuf = state_ssd_conv[li]
        cbuf = state_conf_conv[li]
        ssm0 = state_ssm[li].reshape(bs, n_ssm_heads * hdim, n_state)

        def conv_s(proj):
            return _ssdconv_sample(proj.reshape(ls, bs, ncols_a), sbuf.transpose(1, 0, 2), ssd_conv_w[li],
                                   ssd_conv_b[li], off["xbc"], BF16).reshape(ls * bs, conv_dim)

        def ssd_s(xbc, dt):
            xbc3 = xbc.reshape(ls, bs, conv_dim)
            yp, ea, xw, cd = _ssd_sample_a(xbc3, dt, a_log[li], d_skip[li], groups, heads, hdim, n_state)
            gn = groups * n_state

            def bmajor(v, rows):
                return jnp.pad(v.transpose(1, 0, 2), ((0, 0), (0, rows - ls), (0, 0)))

            r16 = -(-ls // 16) * 16
            r8 = -(-ls // SUBLANES) * SUBLANES
            b_b = bmajor(xbc3[:, :, d_inner:d_inner + gn], r16)
            c_b = bmajor(xbc3[:, :, d_inner + gn:], r16)
            cd_col = cd.transpose(1, 0, 2).reshape(bs, groups * heads, 1)
            y_b, s_new = _ssd_sample_b(ssm0, c_b, b_b, bmajor(xw, r16), bmajor(ea, r8), bmajor(yp, r8),
                                       cd_col, groups, heads, hdim)
            y = y_b[:, :ls].transpose(1, 0, 2).reshape(ls * bs, d_inner).astype(BF16)
            return y, s_new

        def conf_s(proj):
            vc, cnew = _conf_sample(proj.reshape(ls, bs, ncols_b), cbuf, conf_dw_w[li],
                                    conf_dw_b[li], conf_ln_g[li], conf_ln_b[li], off["ca"], off["cg"])
            return vc.reshape(ls * bs, d_conf), cnew

        xs, proj_s, ssm_s, cconv_s = _layer_group(xs, grp_s, prm, dims, ssd_s, conv_s, conf_s,
                                                  final_norm_g, final_norm)
        outs["ssm_s"].append(ssm_s.reshape(bs, n_ssm_heads, hdim, n_state))
        keep_s = min(ls, ks - 1)
        xbc_tail_s = proj_s.reshape(ls, bs, ncols_a)[ls - keep_s:, :, off["xbc"]:].astype(F32).transpose(1, 0, 2)
        outs["sconv_s"].append(jnp.concatenate([sbuf[:, keep_s:], xbc_tail_s], axis=1))
        outs["cconv_s"].append(cconv_s)

    y_prompt = xp.reshape(bp, lp, d)
    y_sample = xs.reshape(ls, bs, d).transpose(1, 0, 2)
    return (y_prompt, y_sample,
            jnp.stack(outs["ssm_p"], 0), jnp.stack(outs["sconv_p"], 0), jnp.stack(outs["cconv_p"], 0),
            jnp.stack(outs["ssm_s"], 0), jnp.stack(outs["sconv_s"], 0), jnp.stack(outs["cconv_s"], 0))
```

```python
import functools

import jax
import jax.numpy as jnp
from jax import lax
from jax.experimental import pallas as pl
from jax.experimental.pallas import tpu as pltpu

F32 = jnp.float32
BF16 = jnp.bfloat16
EPS = 1e-6
PEER_TOPK = 16
SSD_CHUNK = 128
NEG = -3.0e38
V7X_VMEM_LIMIT_BYTES = 56 * 1024 * 1024
SUBLANES = 8

NT_DIMS = (((1,), (1,)), ((), ()))
TN_DIMS = (((0,), (0,)), ((), ()))


def _cp(*sem):
    return pltpu.CompilerParams(dimension_semantics=sem,
                                vmem_limit_bytes=V7X_VMEM_LIMIT_BYTES)


def _tile(n, pref, mult=128):
    best = None
    t = mult
    while t <= min(n, pref):
        if n % t == 0:
            best = t
        t += mult
    return best if best is not None else n


def _silu(x):
    return x * jax.nn.sigmoid(x)


def _split3(a):
    hi = a.astype(BF16)
    r = a - hi.astype(F32)
    mid = r.astype(BF16)
    lo = (r - mid.astype(F32)).astype(BF16)
    return hi, mid, lo


def _dot01_left(m01, a):
    out = None
    for p in _split3(a):
        t = jnp.dot(m01, p, preferred_element_type=F32)
        out = t if out is None else out + t
    return out


def _dot01_right(a, m01):
    out = None
    for p in _split3(a):
        t = jnp.dot(p, m01, preferred_element_type=F32)
        out = t if out is None else out + t
    return out


class _Group:
    def __init__(self, n_tokens, seq_len, per_token, mods):
        self.T = n_tokens
        self.L = seq_len
        self.per_token = per_token
        self.mods = mods

    def mod_spec(self, tm, tn):
        if self.per_token:
            return pl.BlockSpec((1, tm, tn), lambda i, j: (0, i, j))
        tiles_per_batch = self.L // tm
        return pl.BlockSpec((1, 1, tn), lambda i, j: (i // tiles_per_batch, 0, j))

    def row_tile(self, pref):
        return _tile(self.T if self.per_token else self.L, pref, SUBLANES)


def _ada_kernel(c_ref, w_ref, b_ref, o_ref):
    a = _silu(c_ref[...]).astype(BF16)
    o_ref[...] = jnp.dot(a, w_ref[...].astype(BF16), preferred_element_type=F32) + b_ref[...]


def _adaln(c, w, b):
    m, d = c.shape
    n = w.shape[1]
    tn = _tile(n, 1024)
    return pl.pallas_call(
        _ada_kernel,
        grid=(n // tn,),
        in_specs=[pl.BlockSpec((m, d), lambda j: (0, 0)),
                  pl.BlockSpec((d, tn), lambda j: (0, j)),
                  pl.BlockSpec((1, tn), lambda j: (0, j))],
        out_specs=pl.BlockSpec((m, tn), lambda j: (0, j)),
        out_shape=jax.ShapeDtypeStruct((m, n), F32),
        compiler_params=_cp("arbitrary"),
        name="adaln",
    )(c, w, b.reshape(1, n))


def _modnorm_kernel(x_ref, g_ref, sc_ref, sh_ref, o_ref):
    x = x_ref[...]
    y = x * lax.rsqrt(jnp.mean(x * x, axis=-1, keepdims=True) + EPS) * g_ref[...]
    o_ref[...] = (y * (1.0 + sc_ref[0]) + sh_ref[0]).astype(o_ref.dtype)


def _modnorm(x, g, grp, k_scale, k_shift):
    t, d = x.shape
    tm = grp.row_tile(512)
    return pl.pallas_call(
        _modnorm_kernel,
        grid=(t // tm, 1),
        in_specs=[pl.BlockSpec((tm, d), lambda i, j: (i, 0)),
                  pl.BlockSpec((1, d), lambda i, j: (0, 0)),
                  grp.mod_spec(tm, d), grp.mod_spec(tm, d)],
        out_specs=pl.BlockSpec((tm, d), lambda i, j: (i, 0)),
        out_shape=jax.ShapeDtypeStruct((t, d), BF16),
        compiler_params=_cp("arbitrary", "arbitrary"),
        name="modnorm",
    )(x, g.reshape(1, d), grp.mods[k_scale], grp.mods[k_shift])


def _mm_kernel(a_ref, w_ref, *rest, epi, n_extra):
    extras = rest[:n_extra]
    o_ref = rest[n_extra]
    acc = jnp.dot(a_ref[...], w_ref[...], preferred_element_type=F32)
    vals = [e[0] if len(e.shape) == 3 else e[...] for e in extras]
    o_ref[...] = epi(acc, *vals).astype(o_ref.dtype)


def _mm_wcast_kernel(a_ref, w_ref, *rest, epi, n_extra):
    extras = rest[:n_extra]
    o_ref, w_scr = rest[n_extra], rest[n_extra + 1]

    @pl.when(pl.program_id(1) == 0)
    def _():
        w_scr[...] = w_ref[...].astype(BF16)

    acc = jnp.dot(a_ref[...], w_scr[...], preferred_element_type=F32)
    vals = [e[0] if len(e.shape) == 3 else e[...] for e in extras]
    o_ref[...] = epi(acc, *vals).astype(o_ref.dtype)


def _mm(a, w, *, tm, tn, out_dtype, epi=None, extras=(), name="mm", w_cols=None):
    t, k = a.shape
    c0, n = (0, w.shape[1]) if w_cols is None else w_cols
    if epi is None:
        epi = lambda acc: acc
    lane = 128
    if c0 % tn != 0 and (c0 % lane != 0 or tn % lane != 0):
        w, c0 = w[:, c0:c0 + n], 0
    if c0 % tn == 0:
        w_spec = pl.BlockSpec((k, tn), lambda i, j: (0, j + c0 // tn))
    else:
        w_spec = pl.BlockSpec((pl.Element(k), pl.Element(tn)),
                              lambda i, j: (0, pl.multiple_of((c0 // lane + j * (tn // lane)) * lane, lane)))
    in_specs = [pl.BlockSpec((tm, k), lambda i, j: (i, 0)), w_spec] + [s for _, s in extras]
    out_spec = pl.BlockSpec((tm, tn), lambda i, j: (i, j))
    if w.dtype == BF16:
        return pl.pallas_call(
            functools.partial(_mm_kernel, epi=epi, n_extra=len(extras)),
            grid=(t // tm, n // tn),
            in_specs=in_specs, out_specs=out_spec,
            out_shape=jax.ShapeDtypeStruct((t, n), out_dtype),
            compiler_params=_cp("arbitrary", "arbitrary"),
            name=name,
        )(a, w, *[x for x, _ in extras])
    def col_major(spec):
        row_major_map = spec.index_map
        return pl.BlockSpec(spec.block_shape, lambda j, i: row_major_map(i, j))

    return pl.pallas_call(
        functools.partial(_mm_wcast_kernel, epi=epi, n_extra=len(extras)),
        grid=(n // tn, t // tm),
        in_specs=[col_major(s) for s in in_specs], out_specs=col_major(out_spec),
        out_shape=jax.ShapeDtypeStruct((t, n), out_dtype),
        scratch_shapes=[pltpu.VMEM((k, tn), BF16)],
        compiler_params=_cp("arbitrary", "arbitrary"),
        name=name,
    )(a, w, *[x for x, _ in extras])


def _tile_spec(tm, tn, col_off):
    assert col_off % tn == 0
    off = col_off // tn
    return pl.BlockSpec((tm, tn), lambda i, j: (i, j + off))


def _row_spec(tn):
    return pl.BlockSpec((1, tn), lambda i, j: (0, j))


def _ssdconv_prompt_kernel(x_ref, hb_ref, w_ref, b_ref, o_ref, u_scr, *, ks):
    seq = x_ref.shape[1]
    u_scr[0:SUBLANES, :] = hb_ref[0]
    u_scr[SUBLANES:, :] = x_ref[0].astype(F32)
    acc = b_ref[...]
    for k in range(ks):
        off = SUBLANES - (ks - 1) + k
        acc = acc + w_ref[k:k + 1, :] * u_scr[off:off + seq, :]
    o_ref[0] = _silu(acc).astype(o_ref.dtype)


def _ssdconv_prompt(proj3, hb, w, b, col_off, out_dtype):
    nb, seq, _ = proj3.shape
    ks, c = w.shape
    tc = _tile(c, 256)
    assert col_off % tc == 0
    off = col_off // tc
    return pl.pallas_call(
        functools.partial(_ssdconv_prompt_kernel, ks=ks),
        grid=(nb, c // tc),
        in_specs=[pl.BlockSpec((1, seq, tc), lambda bi, ci: (bi, 0, ci + off)),
                  pl.BlockSpec((1, SUBLANES, tc), lambda bi, ci: (bi, 0, ci)),
                  pl.BlockSpec((ks, tc), lambda bi, ci: (0, ci)),
                  pl.BlockSpec((1, tc), lambda bi, ci: (0, ci))],
        out_specs=pl.BlockSpec((1, seq, tc), lambda bi, ci: (bi, 0, ci)),
        out_shape=jax.ShapeDtypeStruct((nb, seq, c), out_dtype),
        scratch_shapes=[pltpu.VMEM((seq + SUBLANES, tc), F32)],
        compiler_params=_cp("arbitrary", "arbitrary"),
        name="ssdconv_prompt",
    )(proj3, hb, w, b.reshape(1, c))


def _ssdconv_sample_kernel(x_ref, buf_ref, w_ref, b_ref, o_ref, *, ks):
    steps = x_ref.shape[0]
    slabs = [buf_ref[m] for m in range(ks - 1)] + [x_ref[t].astype(F32) for t in range(steps)]
    for t in range(steps):
        acc = b_ref[...]
        for k in range(ks):
            acc = acc + w_ref[k:k + 1, :] * slabs[t + k]
        o_ref[t] = _silu(acc).astype(o_ref.dtype)


def _ssdconv_sample(proj3, buf_t, w, b, col_off, out_dtype):
    steps, nb, _ = proj3.shape
    ks, c = w.shape
    tc = _tile(c, 512)
    assert col_off % tc == 0
    off = col_off // tc
    return pl.pallas_call(
        functools.partial(_ssdconv_sample_kernel, ks=ks),
        grid=(c // tc,),
        in_specs=[pl.BlockSpec((steps, nb, tc), lambda ci: (0, 0, ci + off)),
                  pl.BlockSpec((ks - 1, nb, tc), lambda ci: (0, 0, ci)),
                  pl.BlockSpec((ks, tc), lambda ci: (0, ci)),
                  pl.BlockSpec((1, tc), lambda ci: (0, ci))],
        out_specs=pl.BlockSpec((steps, nb, tc), lambda ci: (0, 0, ci)),
        out_shape=jax.ShapeDtypeStruct((steps, nb, c), out_dtype),
        compiler_params=_cp("arbitrary"),
        name="ssdconv_sample",
    )(proj3, buf_t, w, b.reshape(1, c))


def _ssd_prompt_kernel(x_ref, b_ref, c_ref, dt_ref, dtT_ref, arow_ref, acol_ref, dsk_ref, s0_ref,
                       y_ref, sfin_ref, s_scr, *, heads, hdim, gb):
    ci = pl.program_id(2)
    q = x_ref.shape[0]
    n_state = b_ref.shape[1] // gb
    rp = heads * hdim

    @pl.when(ci == 0)
    def _():
        s_scr[...] = s0_ref[0]

    ii = lax.broadcasted_iota(jnp.int32, (q, q), 0)
    jj = lax.broadcasted_iota(jnp.int32, (q, q), 1)
    lower = jnp.where(ii >= jj, 1.0, 0.0).astype(BF16)
    upper = jnp.where(ii <= jj, 1.0, 0.0).astype(BF16)
    causal = ii <= jj

    def rows(v):
        return jnp.concatenate(
            [jnp.broadcast_to(v[r:r + 1, :], (hdim, v.shape[1])) for r in range(heads)], axis=0)

    for gi in range(gb):
        ch = slice(gi * rp, (gi + 1) * rp)
        st = slice(gi * n_state, (gi + 1) * n_state)
        dt = dt_ref[gi]
        dt_t = dtT_ref[gi]
        acum = _dot01_left(lower, dt * arow_ref[gi])
        acum_t = _dot01_right(dt_t * acol_ref[gi], upper)
        x_t = x_ref[:, ch].astype(F32).T
        bm = b_ref[:, st]
        cm = c_ref[:, st]
        scores_t = lax.dot_general(bm, cm, NT_DIMS, preferred_element_type=F32)
        last = acum_t[:, q - 1:q]
        xdt_t = (x_t * rows(dt_t)).astype(BF16)
        ydiag = []
        for r in range(heads):
            seg = acum_t[r:r + 1, :] - acum[:, r:r + 1]
            dec = jnp.exp(jnp.where(causal, seg, NEG))
            wgt = (scores_t * dec).astype(BF16)
            ydiag.append(jnp.dot(xdt_t[r * hdim:(r + 1) * hdim, :], wgt, preferred_element_type=F32))
        s_prev = s_scr[ch, :]
        yoff_t = lax.dot_general(s_prev.astype(BF16), cm, NT_DIMS, preferred_element_type=F32)
        y_t = jnp.concatenate(ydiag, axis=0) + yoff_t * rows(jnp.exp(acum_t)) + dsk_ref[gi] * x_t
        y_ref[:, ch] = y_t.T.astype(y_ref.dtype)

        coef_t = dt_t * jnp.exp(last - acum_t)
        chunk_s = jnp.dot((x_t * rows(coef_t)).astype(BF16), bm, preferred_element_type=F32)
        cd = jnp.broadcast_to(jnp.exp(last), (heads, n_state))
        s_scr[ch, :] = rows(cd) * s_prev + chunk_s

    @pl.when(ci == pl.num_programs(2) - 1)
    def _():
        sfin_ref[0] = s_scr[...]


def _ssd_prompt(xbc, dt, a_log, d_skip, s0, nb, seq, groups, heads, hdim, n_state):
    t = xbc.shape[0]
    q = SSD_CHUNK if seq % SSD_CHUNK == 0 else seq
    nc = seq // q
    rp = heads * hdim
    d_inner = groups * rp
    a = -jnp.exp(a_log.astype(F32))
    dtg = dt.reshape(t, groups, heads).transpose(1, 0, 2)
    dtg_t = dt.reshape(t, groups, heads).transpose(1, 2, 0)
    a_row = a.reshape(groups, 1, heads)
    a_col = a.reshape(groups, heads, 1)
    dsk = jnp.broadcast_to(jnp.repeat(d_skip.astype(F32), hdim).reshape(groups, rp, 1), (groups, rp, q))
    gb = next(g for g in (4, 2, 1) if groups % g == 0)
    ng = groups // gb
    gn = gb * n_state
    assert d_inner % gn == 0 and (groups * n_state) % gn == 0
    b_off = d_inner // gn
    c_off = (d_inner + groups * n_state) // gn
    return pl.pallas_call(
        functools.partial(_ssd_prompt_kernel, heads=heads, hdim=hdim, gb=gb),
        grid=(nb, ng, nc),
        in_specs=[pl.BlockSpec((q, gb * rp), lambda b, g, c: (b * nc + c, g)),
                  pl.BlockSpec((q, gn), lambda b, g, c: (b * nc + c, b_off + g)),
                  pl.BlockSpec((q, gn), lambda b, g, c: (b * nc + c, c_off + g)),
                  pl.BlockSpec((gb, q, heads), lambda b, g, c: (g, b * nc + c, 0)),
                  pl.BlockSpec((gb, heads, q), lambda b, g, c: (g, 0, b * nc + c)),
                  pl.BlockSpec((gb, 1, heads), lambda b, g, c: (g, 0, 0)),
                  pl.BlockSpec((gb, heads, 1), lambda b, g, c: (g, 0, 0)),
                  pl.BlockSpec((gb, rp, q), lambda b, g, c: (g, 0, 0)),
                  pl.BlockSpec((1, gb * rp, n_state), lambda b, g, c: (b, g, 0))],
        out_specs=[pl.BlockSpec((q, gb * rp), lambda b, g, c: (b * nc + c, g)),
                   pl.BlockSpec((1, gb * rp, n_state), lambda b, g, c: (b, g, 0))],
        out_shape=[jax.ShapeDtypeStruct((t, d_inner), BF16),
                   jax.ShapeDtypeStruct((nb, groups * rp, n_state), F32)],
        scratch_shapes=[pltpu.VMEM((gb * rp, n_state), F32)],
        compiler_params=_cp("arbitrary", "arbitrary", "arbitrary"),
        name="ssd_prompt",
    )(xbc, xbc, xbc, dtg, dtg_t, a_row, a_col, dsk, s0)


def _ssd_sample_a_kernel(x_ref, b_ref, c_ref, dt_ref, a_ref, dsk_ref, e_ref,
                         yp_ref, ea_ref, xw_ref, cd_ref):
    steps = x_ref.shape[0]
    e01 = e_ref[...]
    a_row = a_ref[0]
    dts = [dt_ref[0, t] for t in range(steps)]
    acum = []
    for t in range(steps):
        da = dts[t] * a_row
        acum.append(da if t == 0 else acum[-1] + da)
    xs = [x_ref[t].astype(F32) for t in range(steps)]
    bs = [b_ref[t].astype(F32) for t in range(steps)]
    cs = [c_ref[t].astype(F32) for t in range(steps)]
    for t in range(steps):
        acc = dsk_ref[0] * xs[t]
        for j in range(t + 1):
            gsc = jnp.sum(cs[t] * bs[j], axis=-1, keepdims=True)
            wgt = gsc * jnp.exp(acum[t] - acum[j]) * dts[j]
            acc = acc + _dot01_right(wgt, e01) * xs[j]
        yp_ref[t] = acc
        ea_ref[t] = _dot01_right(jnp.exp(acum[t]), e01)
        xw_ref[t] = (_dot01_right(dts[t] * jnp.exp(acum[steps - 1] - acum[t]), e01) * xs[t]).astype(xw_ref.dtype)
    cd_ref[0] = jnp.exp(acum[steps - 1])


def _ssd_sample_a(xbc3, dt, a_log, d_skip, groups, heads, hdim, n_state):
    steps, nb, _ = xbc3.shape
    rp = heads * hdim
    d_inner = groups * rp
    a = -jnp.exp(a_log.astype(F32)).reshape(groups, 1, heads)
    dtg = dt.reshape(steps, nb, groups, heads).transpose(2, 0, 1, 3)
    dsk = jnp.repeat(d_skip.astype(F32), hdim).reshape(groups, 1, rp)
    e01 = (jnp.arange(rp)[None, :] // hdim == jnp.arange(heads)[:, None]).astype(BF16)
    b_off = d_inner // n_state
    c_off = (d_inner + groups * n_state) // n_state
    big = pl.BlockSpec((steps, nb, rp), lambda g: (0, 0, g))
    return pl.pallas_call(
        _ssd_sample_a_kernel,
        grid=(groups,),
        in_specs=[big,
                  pl.BlockSpec((steps, nb, n_state), lambda g: (0, 0, b_off + g)),
                  pl.BlockSpec((steps, nb, n_state), lambda g: (0, 0, c_off + g)),
                  pl.BlockSpec((1, steps, nb, heads), lambda g: (g, 0, 0, 0)),
                  pl.BlockSpec((1, 1, heads), lambda g: (g, 0, 0)),
                  pl.BlockSpec((1, 1, rp), lambda g: (g, 0, 0)),
                  pl.BlockSpec((heads, rp), lambda g: (0, 0))],
        out_specs=[big, big, big, pl.BlockSpec((1, nb, heads), lambda g: (g, 0, 0))],
        out_shape=[jax.ShapeDtypeStruct((steps, nb, d_inner), F32),
                   jax.ShapeDtypeStruct((steps, nb, d_inner), F32),
                   jax.ShapeDtypeStruct((steps, nb, d_inner), BF16),
                   jax.ShapeDtypeStruct((groups, nb, heads), F32)],
        compiler_params=_cp("arbitrary"),
        name="ssd_sample_a",
    )(xbc3, xbc3, xbc3, dtg, a, dsk, e01)


def _ssd_sample_b_kernel(s_ref, c_ref, b_ref, xw_ref, ea_ref, yp_ref, cd_ref, y_ref, so_ref,
                         *, groups, heads, hdim):
    rp = heads * hdim
    n_state = s_ref.shape[2]
    rows8 = yp_ref.shape[1]
    cdl = jnp.broadcast_to(cd_ref[0], (groups * heads, n_state))
    for g in range(groups):
        s0 = s_ref[0, g * rp:(g + 1) * rp, :]
        cg = c_ref[0, :, g * n_state:(g + 1) * n_state]
        bg = b_ref[0, :, g * n_state:(g + 1) * n_state]
        yoff = lax.dot_general(cg, s0.astype(BF16), NT_DIMS, preferred_element_type=F32)
        cols = slice(g * rp, (g + 1) * rp)
        y_ref[0, :, cols] = yp_ref[0, :, cols] + ea_ref[0, :, cols] * yoff[:rows8]
        upd = lax.dot_general(xw_ref[0, :, cols], bg, TN_DIMS, preferred_element_type=F32)
        cdrows = jnp.concatenate(
            [jnp.broadcast_to(cdl[g * heads + r:g * heads + r + 1, :], (hdim, n_state))
             for r in range(heads)], axis=0)
        so_ref[0, g * rp:(g + 1) * rp, :] = cdrows * s0 + upd


def _ssd_sample_b(s0, c_b, b_b, xw_b, ea_b, yp_b, cd_col, groups, heads, hdim):
    nb, hp, n_state = s0.shape
    d_inner = hp
    r16 = c_b.shape[1]
    r8 = yp_b.shape[1]
    gn = c_b.shape[2]
    return pl.pallas_call(
        functools.partial(_ssd_sample_b_kernel, groups=groups, heads=heads, hdim=hdim),
        grid=(nb,),
        in_specs=[pl.BlockSpec((1, hp, n_state), lambda b: (b, 0, 0)),
                  pl.BlockSpec((1, r16, gn), lambda b: (b, 0, 0)),
                  pl.BlockSpec((1, r16, gn), lambda b: (b, 0, 0)),
                  pl.BlockSpec((1, r16, d_inner), lambda b: (b, 0, 0)),
                  pl.BlockSpec((1, r8, d_inner), lambda b: (b, 0, 0)),
                  pl.BlockSpec((1, r8, d_inner), lambda b: (b, 0, 0)),
                  pl.BlockSpec((1, groups * heads, 1), lambda b: (b, 0, 0))],
        out_specs=[pl.BlockSpec((1, r8, d_inner), lambda b: (b, 0, 0)),
                   pl.BlockSpec((1, hp, n_state), lambda b: (b, 0, 0))],
        out_shape=[jax.ShapeDtypeStruct((nb, r8, d_inner), F32),
                   jax.ShapeDtypeStruct((nb, hp, n_state), F32)],
        compiler_params=_cp("arbitrary"),
        name="ssd_sample_b",
    )(s0, c_b, b_b, xw_b, ea_b, yp_b, cd_col)


def _gatednorm_kernel(y_ref, z_ref, g_ref, o_ref):
    v = y_ref[...].astype(F32) * _silu(z_ref[...].astype(F32))
    o_ref[...] = (v * lax.rsqrt(jnp.mean(v * v, axis=-1, keepdims=True) + EPS) * g_ref[...]).astype(o_ref.dtype)


def _gatednorm(y, proj, z_off, g, tm):
    t, d = y.shape
    assert z_off % d == 0
    zo = z_off // d
    return pl.pallas_call(
        _gatednorm_kernel,
        grid=(t // tm,),
        in_specs=[pl.BlockSpec((tm, d), lambda i: (i, 0)),
                  pl.BlockSpec((tm, d), lambda i: (i, zo)),
                  pl.BlockSpec((1, d), lambda i: (0, 0))],
        out_specs=pl.BlockSpec((tm, d), lambda i: (i, 0)),
        out_shape=jax.ShapeDtypeStruct((t, d), BF16),
        compiler_params=_cp("arbitrary"),
        name="gatednorm",
    )(y, proj, g.reshape(1, d))


CONF_HALO = 32


def _layernorm_silu(v, g, b):
    mu = jnp.mean(v, axis=-1, keepdims=True)
    vc = v - mu
    var = jnp.mean(vc * vc, axis=-1, keepdims=True)
    return _silu(vc * lax.rsqrt(var + EPS) * g + b)


def _conf_prompt_kernel(ca_ref, cg_ref, hb_ref, w_ref, b_ref, g_ref, lb_ref, o_ref, tail_ref,
                        u_scr, v_scr, sh_scr, *, kc, lane_chunk):
    li = pl.program_id(1)
    tl = ca_ref.shape[1]
    d = ca_ref.shape[2]

    @pl.when(li == 0)
    def _():
        u_scr[0:CONF_HALO, :] = hb_ref[0]

    @pl.when(li > 0)
    def _():
        u_scr[0:CONF_HALO, :] = u_scr[tl:tl + CONF_HALO, :]

    u_scr[CONF_HALO:, :] = ca_ref[0].astype(F32) * jax.nn.sigmoid(cg_ref[0].astype(F32))

    span = tl + CONF_HALO - SUBLANES

    def chunk(ci, carry):
        lanes = pl.ds(pl.multiple_of(ci * lane_chunk, lane_chunk), lane_chunk)
        for r in range(1, SUBLANES):
            sh_scr[r - 1] = u_scr[r:r + span, lanes]
        acc = jnp.broadcast_to(b_ref[:, lanes], (tl, lane_chunk))
        for k in range(kc):
            a, r = divmod(CONF_HALO - (kc - 1) + k, SUBLANES)
            rows = pl.ds(a * SUBLANES, tl)
            src = u_scr[rows, lanes] if r == 0 else sh_scr[r - 1, rows, :]
            acc = acc + w_ref[k:k + 1, lanes] * src
        v_scr[:, lanes] = acc
        return carry

    lax.fori_loop(0, d // lane_chunk, chunk, 0)
    o_ref[...] = _layernorm_silu(v_scr[...], g_ref[...], lb_ref[...]).astype(o_ref.dtype)

    @pl.when(li == pl.num_programs(1) - 1)
    def _():
        tail_ref[0] = u_scr[tl:tl + CONF_HALO, :]


def _conf_prompt(proj3, hb, w, b, ln_g, ln_b, ca_off, cg_off):
    nb, seq, _ = proj3.shape
    kc, d = w.shape
    tl = _tile(seq, 128, CONF_HALO)
    nl = seq // tl
    assert ca_off % d == 0 and cg_off % d == 0 and tl >= CONF_HALO and kc - 1 <= CONF_HALO
    cao, cgo = ca_off // d, cg_off // d
    lane_chunk = _tile(d, 512)
    row = lambda bi, li: (0, 0)
    return pl.pallas_call(
        functools.partial(_conf_prompt_kernel, kc=kc, lane_chunk=lane_chunk),
        grid=(nb, nl),
        in_specs=[pl.BlockSpec((1, tl, d), lambda bi, li: (bi, li, cao)),
                  pl.BlockSpec((1, tl, d), lambda bi, li: (bi, li, cgo)),
                  pl.BlockSpec((1, CONF_HALO, d), lambda bi, li: (bi, 0, 0)),
                  pl.BlockSpec((kc, d), row),
                  pl.BlockSpec((1, d), row), pl.BlockSpec((1, d), row), pl.BlockSpec((1, d), row)],
        out_specs=[pl.BlockSpec((tl, d), lambda bi, li: (bi * nl + li, 0)),
                   pl.BlockSpec((1, CONF_HALO, d), lambda bi, li: (bi, 0, 0))],
        out_shape=[jax.ShapeDtypeStruct((nb * seq, d), BF16),
                   jax.ShapeDtypeStruct((nb, CONF_HALO, d), F32)],
        scratch_shapes=[pltpu.VMEM((tl + CONF_HALO, d), F32), pltpu.VMEM((tl, d), F32),
                        pltpu.VMEM((SUBLANES - 1, tl + CONF_HALO - SUBLANES, lane_chunk), F32)],
        compiler_params=_cp("arbitrary", "arbitrary"),
        name="conf_prompt",
    )(proj3, proj3, hb, w, b.reshape(1, d), ln_g.reshape(1, d), ln_b.reshape(1, d))


def _conf_sample_kernel(ca_ref, cg_ref, buf_ref, w_ref, b_ref, g_ref, lb_ref, o_ref, so_ref,
                        xp_scr, v_scr, *, kc):
    steps, bb, _ = ca_ref.shape
    win = w_ref.shape[0]
    nbuf = kc - 1
    ucs = [ca_ref[t].astype(F32) * jax.nn.sigmoid(cg_ref[t].astype(F32)) for t in range(steps)]
    xp_scr[nbuf + steps:, :] = jnp.zeros((xp_scr.shape[0] - nbuf - steps, xp_scr.shape[1]), F32)
    for b in range(bb):
        xp_scr[0:nbuf, :] = buf_ref[b]
        for t in range(steps):
            xp_scr[nbuf + t:nbuf + t + 1, :] = ucs[t][b:b + 1, :]
        for t in range(steps):
            v_scr[t, b:b + 1, :] = jnp.sum(w_ref[...] * xp_scr[t:t + win, :], axis=0, keepdims=True)
        so_ref[b] = xp_scr[steps:steps + nbuf, :]
    for t in range(steps):
        o_ref[t] = _layernorm_silu(v_scr[t] + b_ref[...], g_ref[...], lb_ref[...]).astype(o_ref.dtype)


def _conf_sample(proj3, buf, w, b, ln_g, ln_b, ca_off, cg_off):
    steps, nb, _ = proj3.shape
    kc, d = w.shape
    assert steps < kc - 1
    bb = SUBLANES
    win = -(-kc // SUBLANES) * SUBLANES
    rows = -(-(steps + win) // SUBLANES) * SUBLANES
    w_pad = jnp.pad(w, ((0, win - kc), (0, 0)))
    cao, cgo = ca_off // d, cg_off // d
    row = lambda i: (0, 0)
    return pl.pallas_call(
        functools.partial(_conf_sample_kernel, kc=kc),
        grid=(nb // bb,),
        in_specs=[pl.BlockSpec((steps, bb, d), lambda i: (0, i, cao)),
                  pl.BlockSpec((steps, bb, d), lambda i: (0, i, cgo)),
                  pl.BlockSpec((bb, kc - 1, d), lambda i: (i, 0, 0)),
                  pl.BlockSpec((win, d), row),
                  pl.BlockSpec((1, d), row), pl.BlockSpec((1, d), row), pl.BlockSpec((1, d), row)],
        out_specs=[pl.BlockSpec((steps, bb, d), lambda i: (0, i, 0)),
                   pl.BlockSpec((bb, kc - 1, d), lambda i: (i, 0, 0))],
        out_shape=[jax.ShapeDtypeStruct((steps, nb, d), BF16),
                   jax.ShapeDtypeStruct((nb, kc - 1, d), F32)],
        scratch_shapes=[pltpu.VMEM((rows, d), F32), pltpu.VMEM((steps, bb, d), F32)],
        compiler_params=_cp("arbitrary"),
        name="conf_sample",
    )(proj3, proj3, buf, w_pad, b.reshape(1, d), ln_g.reshape(1, d), ln_b.reshape(1, d))


def _top_values(s, k, scr):
    cur = s
    for r in range(k):
        m = jnp.max(cur, axis=0, keepdims=True)
        scr[r:r + 1, :] = m
        cur = jnp.where(cur >= m, NEG, cur)
    return scr[...]


def _candidate_sums(v1, v2, k):
    slabs = []
    a = 0
    while a < k and k // (a + 1) > 1:
        cnt = k // (a + 1)
        rows = -(-cnt // SUBLANES) * SUBLANES
        s = v1[a:a + 1, :] + v2[0:rows, :]
        if cnt < rows:
            s = jnp.where(lax.broadcasted_iota(jnp.int32, s.shape, 0) < cnt, s, NEG)
        slabs.append(s)
        a += 1
    if a < k:
        slabs.append(v1[a:k, :] + v2[0:1, :])
    return jnp.concatenate(slabs, axis=0)


def _peer_topk_kernel(q_ref, k1_ref, k2_ref, thr_ref, e1_ref, s2_ref, e2_ref, v1_scr, v2_scr,
                      *, n_heads, half, topk):
    for h in range(n_heads):
        base = h * 2 * half
        q1 = q_ref[:, base:base + half].astype(BF16)
        q2 = q_ref[:, base + half:base + 2 * half].astype(BF16)
        s1 = lax.dot_general(k1_ref[h].astype(BF16), q1, NT_DIMS, preferred_element_type=F32)
        s2 = lax.dot_general(k2_ref[h].astype(BF16), q2, NT_DIMS, preferred_element_type=F32)
        v1 = _top_values(s1, topk, v1_scr)
        v2 = _top_values(s2, topk, v2_scr)
        cand = _candidate_sums(v1, v2, topk)
        cur = cand
        tau = None
        for r in range(topk):
            tau = jnp.max(cur, axis=0, keepdims=True)
            if r < topk - 1:
                cur = jnp.where(cur >= tau, NEG, cur)
        top = v1[0:1, :] + v2[0:1, :]
        z = jnp.sum(jnp.where(cand >= tau, jnp.exp(cand - top), 0.0), axis=0, keepdims=True)
        thr_ref[h] = tau - s1
        e1_ref[h] = jnp.exp(s1 - v1[0:1, :]) / z
        s2_ref[h] = s2
        e2_ref[h] = jnp.exp(s2 - v2[0:1, :])


def _peer_topk(qv, keys1, keys2):
    t, _ = qv.shape
    n_heads, nk, half = keys1.shape
    tq = _tile(t, 256)
    kspec = pl.BlockSpec((n_heads, nk, half), lambda i: (0, 0, 0))
    ospec = pl.BlockSpec((n_heads, nk, tq), lambda i: (0, 0, i))
    oshape = jax.ShapeDtypeStruct((n_heads, nk, t), F32)
    return pl.pallas_call(
        functools.partial(_peer_topk_kernel, n_heads=n_heads, half=half, topk=PEER_TOPK),
        grid=(t // tq,),
        in_specs=[pl.BlockSpec((tq, n_heads * 2 * half), lambda i: (i, 0)), kspec, kspec],
        out_specs=[ospec] * 4,
        out_shape=[oshape] * 4,
        scratch_shapes=[pltpu.VMEM((PEER_TOPK, tq), F32), pltpu.VMEM((PEER_TOPK, tq), F32)],
        compiler_params=_cp("arbitrary"),
        name="peer_topk",
    )(qv, keys1, keys2)


def _gelu_exact(x):
    return 0.5 * x * (1.0 + lax.erf(x * 0.7071067811865476))


def _peer_dense_kernel(h_ref, u_ref, v_ref, thr_ref, e1_ref, s2_ref, e2_ref, o_ref, a_scr, g_scr, act_scr,
                       *, n_heads, nk):
    j = pl.program_id(1)
    n_blocks = pl.num_programs(1) - 1
    te = u_ref.shape[0]
    tm = h_ref.shape[0]
    d = o_ref.shape[1]
    dc = _tile(d, 512)
    tk = _tile(tm, 256)
    lt = _tile(tm, 128)

    def gate_piece(jb, a, c, key_rows):
        if (a, 0) not in key_rows:
            i1 = jb * (te // nk) + a
            for h in range(n_heads):
                key_rows[(a, h)] = (thr_ref[h, pl.ds(i1, 1), :], e1_ref[h, pl.ds(i1, 1), :])
        tok = slice(c * lt, (c + 1) * lt)
        w = jnp.zeros((nk, lt), F32)
        for h in range(n_heads):
            thr, e1 = key_rows[(a, h)]
            w = w + jnp.where(s2_ref[h, :, tok] >= thr[:, tok], e2_ref[h, :, tok] * e1[:, tok], 0.0)
        g_scr[a * nk:(a + 1) * nk, tok] = w

    def step(jb, slot, a_prev):
        pieces = [(a, c) for a in range(te // nk) for c in range(tm // lt)] if jb is not None else []
        n_mm = (tm // tk if jb is not None else 0) + (d // dc if a_prev is not None else 0)
        per = -(-len(pieces) // max(n_mm, 1))
        key_rows = {}

        def some_pieces():
            for _ in range(per):
                if pieces:
                    gate_piece(jb, *pieces.pop(0), key_rows)

        if jb is not None:
            for hf in range(tm // tk):
                tok = slice(hf * tk, (hf + 1) * tk)
                act_scr[:, tok] = _gelu_exact(lax.dot_general(u_ref[...], h_ref[tok, :], NT_DIMS,
                                                              preferred_element_type=F32))
                some_pieces()
        if a_prev is not None:
            for c in range(d // dc):
                cols = slice(c * dc, (c + 1) * dc)
                o_ref[:, cols] += jnp.dot(a_scr[a_prev], v_ref[:, cols], preferred_element_type=F32)
                some_pieces()
        if jb is not None:
            for c in range(tm // lt):
                tok = slice(c * lt, (c + 1) * lt)
                a_scr[slot, tok, :] = (act_scr[:, tok] * g_scr[:, tok]).T.astype(BF16)

    @pl.when(j == 0)
    def _():
        o_ref[...] = jnp.zeros_like(o_ref)
        step(0, 0, None)

    @pl.when(jnp.logical_and(j > 0, j < n_blocks))
    def _():
        slot = lax.rem(j, 2)
        step(j, slot, 1 - slot)

    @pl.when(j == n_blocks)
    def _():
        step(None, None, lax.rem(n_blocks - 1, 2))


def _peer_dense(h2, u, v, thr, e1, s2, e2):
    t, d = h2.shape
    n_exp = u.shape[0]
    n_heads, nk, _ = thr.shape
    tm = _tile(t, 512)
    te = 4 * nk
    n_blocks = n_exp // te
    once = pl.Buffered(1)
    fspec = pl.BlockSpec((n_heads, nk, tm), lambda i, j: (0, 0, i), pipeline_mode=once)
    return pl.pallas_call(
        functools.partial(_peer_dense_kernel, n_heads=n_heads, nk=nk),
        grid=(t // tm, n_blocks + 1),
        in_specs=[pl.BlockSpec((tm, d), lambda i, j: (i, 0), pipeline_mode=once),
                  pl.BlockSpec((te, d), lambda i, j: (jnp.minimum(j, n_blocks - 1), 0)),
                  pl.BlockSpec((te, d), lambda i, j: (jnp.maximum(j - 1, 0), 0)),
                  fspec, fspec, fspec, fspec],
        out_specs=pl.BlockSpec((tm, d), lambda i, j: (i, 0)),
        out_shape=jax.ShapeDtypeStruct((t, d), F32),
        scratch_shapes=[pltpu.VMEM((2, tm, te), BF16), pltpu.VMEM((te, tm), F32), pltpu.VMEM((te, tm), F32)],
        compiler_params=_cp("arbitrary", "arbitrary"),
        name="peer_dense",
    )(h2, u, v, thr, e1, s2, e2)


def _residual_kernel(x_ref, p_ref, gate_ref, g_ref, o_ref, *, final_norm):
    x = x_ref[...] + gate_ref[0] * p_ref[...]
    if final_norm:
        x = x * lax.rsqrt(jnp.mean(x * x, axis=-1, keepdims=True) + EPS) * g_ref[...]
    o_ref[...] = x


def _residual(x1, pe, grp, k_gate, g_final, final_norm):
    t, d = x1.shape
    tm = grp.row_tile(256)
    return pl.pallas_call(
        functools.partial(_residual_kernel, final_norm=final_norm),
        grid=(t // tm, 1),
        in_specs=[pl.BlockSpec((tm, d), lambda i, j: (i, 0)),
                  pl.BlockSpec((tm, d), lambda i, j: (i, 0)),
                  grp.mod_spec(tm, d),
                  pl.BlockSpec((1, d), lambda i, j: (0, 0))],
        out_specs=pl.BlockSpec((tm, d), lambda i, j: (i, 0)),
        out_shape=jax.ShapeDtypeStruct((t, d), F32),
        compiler_params=_cp("arbitrary", "arbitrary"),
        name="residual",
    )(x1, pe, grp.mods[k_gate], g_final.reshape(1, d))


def _softplus(x):
    return jnp.maximum(x, 0.0) + jnp.log1p(jnp.exp(-jnp.abs(x)))


def _layer_group(x, grp, prm, dims, ssd_fn, conv_fn, conf_fn, final_g, final_norm):
    d = dims["d"]
    d_inner = dims["d_inner"]
    off = dims["off"]
    tm = grp.row_tile(1024)
    tn = 1024

    h = _modnorm(x, prm["norm1_g"], grp, 1, 0)
    (a0, na), (b0, nb) = dims["cols_a"], dims["cols_b"]
    tn_in = 512
    proj_a = _mm(h, prm["w_in"], tm=tm, tn=_tile(na, tn_in), out_dtype=BF16, name="in_proj_a", w_cols=(a0, na))
    proj_b = _mm(h, prm["w_in"], tm=tm, tn=_tile(nb, tn_in), out_dtype=BF16, name="in_proj_b", w_cols=(b0, nb))
    hs = prm["w_dt"].shape[1]
    dt = _mm(h, prm["w_dt"], tm=tm, tn=hs, out_dtype=F32,
             epi=lambda acc, bias: _softplus(acc + bias),
             extras=[(prm["dt_bias"].reshape(1, hs), _row_spec(hs))], name="dt_proj")

    xbc = conv_fn(proj_a)
    y, ssm_new = ssd_fn(xbc, dt)
    yn = _gatednorm(y, proj_a, off["z"], prm["ssd_norm_g"], grp.row_tile(256))
    tn_d = _tile(d, 512)
    sa = _mm(yn, prm["w_ssd_out"], tm=grp.row_tile(512), tn=tn_d, out_dtype=BF16,
             epi=lambda acc, ga: jax.nn.sigmoid(ga.astype(F32)) * acc,
             extras=[(proj_b, _tile_spec(grp.row_tile(512), tn_d, off["ga"]))], name="ssd_out")

    vc, conf_state = conf_fn(proj_b)
    tmm = grp.row_tile(512)
    tn_m = _tile(d, 1024)
    mixed = _mm(vc, prm["w_conf_out"], tm=tmm, tn=tn_m, out_dtype=BF16,
                epi=lambda acc, s, gb: s.astype(F32) + jax.nn.sigmoid(gb.astype(F32)) * acc,
                extras=[(sa, _tile_spec(tmm, tn_m, 0)), (proj_b, _tile_spec(tmm, tn_m, off["gb"]))],
                name="conf_out")
    x1 = _mm(mixed, prm["w_out"], tm=tmm, tn=tn_m, out_dtype=F32,
             epi=lambda acc, xr, gate: xr + gate * acc,
             extras=[(x, _tile_spec(tmm, tn_m, 0)), (grp.mods[2], grp.mod_spec(tmm, tn_m))],
             name="out_proj")

    h2 = _modnorm(x1, prm["norm2_g"], grp, 4, 3)
    qv = _mm(h2, prm["peer_wq"], tm=tmm, tn=_tile(prm["peer_wq"].shape[1], 1024), out_dtype=F32, name="peer_q")
    thr, e1, s2, e2 = _peer_topk(qv, prm["peer_keys1"], prm["peer_keys2"])
    pe = _peer_dense(h2, prm["peer_u"], prm["peer_v"], thr, e1, s2, e2)
    x2 = _residual(x1, pe, grp, 5, final_g, final_norm)
    return x2, proj_a, ssm_new, conf_state


def kernel(x_prompt, x_sample, c_prompt, c_sample, state_ssm, state_ssd_conv, state_conf_conv, w_ada, b_ada, norm1_g, w_in, ssd_conv_w, ssd_conv_b, dt_bias, a_log, d_skip, ssd_norm_g, w_ssd_out, conf_dw_w, conf_dw_b, conf_ln_g, conf_ln_b, w_conf_out, w_out, norm2_g, peer_wq, peer_keys1, peer_keys2, peer_u, peer_v, final_norm_g):
    depth = w_ada.shape[0]
    bp, lp, d = x_prompt.shape
    bs, ls, _ = x_sample.shape
    n_state = state_ssm.shape[-1]
    hdim = state_ssm.shape[-2]
    n_ssm_heads = a_log.shape[-1]
    d_inner = ssd_norm_g.shape[-1]
    conv_dim = ssd_conv_w.shape[-1]
    groups = (conv_dim - d_inner) // (2 * n_state)
    heads = n_ssm_heads // groups
    d_conf = conf_dw_w.shape[-1]
    kc = conf_dw_w.shape[-2]
    ks = ssd_conv_w.shape[-2]
    n_mod = w_ada.shape[-1] // d

    c_xbc, c_dt = d_inner + conv_dim, d_inner + conv_dim + n_ssm_heads
    off = {"z": 0, "xbc": d_inner,
           "ca": 0, "cg": d_conf, "ga": 2 * d_conf, "gb": 2 * d_conf + d}
    ncols_a, ncols_b = c_xbc, 2 * d_conf + 2 * d
    dims = {"d": d, "d_inner": d_inner, "off": off, "cols_a": (0, ncols_a), "cols_b": (c_dt, ncols_b)}

    xp = x_prompt.reshape(bp * lp, d)
    xs = x_sample.transpose(1, 0, 2).reshape(ls * bs, d)
    rows_c = bp + bs
    rows_pad = -(-rows_c // SUBLANES) * SUBLANES
    c_all = jnp.pad(jnp.concatenate([c_prompt, c_sample], axis=0), ((0, rows_pad - rows_c), (0, 0)))

    outs = {k: [] for k in ("ssm_p", "sconv_p", "cconv_p", "ssm_s", "sconv_s", "cconv_s")}
    for li in range(depth):
        final_norm = li == depth - 1
        mod = _adaln(c_all, w_ada[li], b_ada[li])
        mod_p = mod[:bp].reshape(bp, n_mod, d)
        mod_s = mod[bp:bp + bs].reshape(bs, n_mod, d)
        grp_p = _Group(bp * lp, lp, False, [mod_p[:, k][:, None, :] for k in range(n_mod)])
        grp_s = _Group(ls * bs, ls, True, [jnp.tile(mod_s[:, k], (ls, 1))[None] for k in range(n_mod)])

        wi = w_in[li]
        prm = {
            "norm1_g": norm1_g[li], "norm2_g": norm2_g[li], "ssd_norm_g": ssd_norm_g[li],
            "w_in": wi, "w_dt": wi[:, c_xbc:c_dt].astype(BF16), "dt_bias": dt_bias[li].astype(F32),
            "w_ssd_out": w_ssd_out[li].astype(BF16), "w_conf_out": w_conf_out[li].astype(BF16),
            "w_out": w_out[li].astype(BF16), "peer_wq": peer_wq[li].astype(BF16),
            "peer_keys1": peer_keys1[li], "peer_keys2": peer_keys2[li],
            "peer_u": peer_u[li].astype(BF16), "peer_v": peer_v[li].astype(BF16),
        }

        def conv_p(proj):
            hb = jnp.zeros((bp, SUBLANES, conv_dim), F32)
            return _ssdconv_prompt(proj.reshape(bp, lp, ncols_a), hb, ssd_conv_w[li], ssd_conv_b[li],
                                   off["xbc"], BF16).reshape(bp * lp, conv_dim)

        def ssd_p(xbc, dt):
            s0 = jnp.zeros((bp, n_ssm_heads * hdim, n_state), F32)
            return _ssd_prompt(xbc, dt, a_log[li], d_skip[li], s0, bp, lp, groups, heads, hdim, n_state)

        def conf_p(proj):
            hb = jnp.zeros((bp, CONF_HALO, d_conf), F32)
            return _conf_prompt(proj.reshape(bp, lp, ncols_b), hb, conf_dw_w[li], conf_dw_b[li],
                                conf_ln_g[li], conf_ln_b[li], off["ca"], off["cg"])

        xp, proj_p, ssm_p, tail_p = _layer_group(xp, grp_p, prm, dims, ssd_p, conv_p, conf_p,
                                                 final_norm_g, final_norm)
        outs["ssm_p"].append(ssm_p.reshape(bp, n_ssm_heads, hdim, n_state))
        keep_p = min(lp, ks - 1)
        xbc_tail_p = proj_p.reshape(bp, lp, ncols_a)[:, lp - keep_p:, off["xbc"]:].astype(F32)
        sconv0 = jnp.zeros((bp, ks - 1 - keep_p, conv_dim), F32)
        outs["sconv_p"].append(jnp.concatenate([sconv0, xbc_tail_p], axis=1))
        outs["cconv_p"].append(tail_p[:, CONF_HALO - (kc - 1):])

        sbuf = state_ssd_conv[li]
        cbuf = state_conf_conv[li]
        ssm0 = state_ssm[li].reshape(bs, n_ssm_heads * hdim, n_state)

        def conv_s(proj):
            return _ssdconv_sample(proj.reshape(ls, bs, ncols_a), sbuf.transpose(1, 0, 2), ssd_conv_w[li],
                                   ssd_conv_b[li], off["xbc"], BF16).reshape(ls * bs, conv_dim)

        def ssd_s(xbc, dt):
            xbc3 = xbc.reshape(ls, bs, conv_dim)
            yp, ea, xw, cd = _ssd_sample_a(xbc3, dt, a_log[li], d_skip[li], groups, heads, hdim, n_state)
            gn = groups * n_state

            def bmajor(v, rows):
                return jnp.pad(v.transpose(1, 0, 2), ((0, 0), (0, rows - ls), (0, 0)))

            r16 = -(-ls // 16) * 16
            r8 = -(-ls // SUBLANES) * SUBLANES
            b_b = bmajor(xbc3[:, :, d_inner:d_inner + gn], r16)
            c_b = bmajor(xbc3[:, :, d_inner + gn:], r16)
            cd_col = cd.transpose(1, 0, 2).reshape(bs, groups * heads, 1)
            y_b, s_new = _ssd_sample_b(ssm0, c_b, b_b, bmajor(xw, r16), bmajor(ea, r8), bmajor(yp, r8),
                                       cd_col, groups, heads, hdim)
            y = y_b[:, :ls].transpose(1, 0, 2).reshape(ls * bs, d_inner).astype(BF16)
            return y, s_new

        def conf_s(proj):
            vc, cnew = _conf_sample(proj.reshape(ls, bs, ncols_b), cbuf, conf_dw_w[li],
                                    conf_dw_b[li], conf_ln_g[li], conf_ln_b[li], off["ca"], off["cg"])
            return vc.reshape(ls * bs, d_conf), cnew

        xs, proj_s, ssm_s, cconv_s = _layer_group(xs, grp_s, prm, dims, ssd_s, conv_s, conf_s,
                                                  final_norm_g, final_norm)
        outs["ssm_s"].append(ssm_s.reshape(bs, n_ssm_heads, hdim, n_state))
        keep_s = min(ls, ks - 1)
        xbc_tail_s = proj_s.reshape(ls, bs, ncols_a)[ls - keep_s:, :, off["xbc"]:].astype(F32).transpose(1, 0, 2)
        outs["sconv_s"].append(jnp.concatenate([sbuf[:, keep_s:], xbc_tail_s], axis=1))
        outs["cconv_s"].append(cconv_s)

    y_prompt = xp.reshape(bp, lp, d)
    y_sample = xs.reshape(ls, bs, d).transpose(1, 0, 2)
    return (y_prompt, y_sample,
            jnp.stack(outs["ssm_p"], 0), jnp.stack(outs["sconv_p"], 0), jnp.stack(outs["cconv_p"], 0),
            jnp.stack(outs["ssm_s"], 0), jnp.stack(outs["sconv_s"], 0), jnp.stack(outs["cconv_s"], 0))
```

```python
import functools

import jax
import jax.numpy as jnp
from jax import lax
from jax.experimental import pallas as pl
from jax.experimental.pallas import tpu as pltpu

F32 = jnp.float32
BF16 = jnp.bfloat16
EPS = 1e-6
PEER_TOPK = 16
SSD_CHUNK = 128
NEG = -3.0e38
V7X_VMEM_LIMIT_BYTES = 56 * 1024 * 1024
SUBLANES = 8

NT_DIMS = (((1,), (1,)), ((), ()))
TN_DIMS = (((0,), (0,)), ((), ()))


def _cp(*sem):
    return pltpu.CompilerParams(dimension_semantics=sem,
                                vmem_limit_bytes=V7X_VMEM_LIMIT_BYTES)


def _tile(n, pref, mult=128):
    best = None
    t = mult
    while t <= min(n, pref):
        if n % t == 0:
            best = t
        t += mult
    return best if best is not None else n


def _silu(x):
    return x * jax.nn.sigmoid(x)


def _split3(a):
    hi = a.astype(BF16)
    r = a - hi.astype(F32)
    mid = r.astype(BF16)
    lo = (r - mid.astype(F32)).astype(BF16)
    return hi, mid, lo


def _dot01_left(m01, a):
    out = None
    for p in _split3(a):
        t = jnp.dot(m01, p, preferred_element_type=F32)
        out = t if out is None else out + t
    return out


def _dot01_right(a, m01):
    out = None
    for p in _split3(a):
        t = jnp.dot(p, m01, preferred_element_type=F32)
        out = t if out is None else out + t
    return out


class _Group:
    def __init__(self, n_tokens, seq_len, per_token, mods):
        self.T = n_tokens
        self.L = seq_len
        self.per_token = per_token
        self.mods = mods

    def mod_spec(self, tm, tn):
        if self.per_token:
            return pl.BlockSpec((1, tm, tn), lambda i, j: (0, i, j))
        tiles_per_batch = self.L // tm
        return pl.BlockSpec((1, 1, tn), lambda i, j: (i // tiles_per_batch, 0, j))

    def row_tile(self, pref):
        return _tile(self.T if self.per_token else self.L, pref, SUBLANES)


def _ada_kernel(c_ref, w_ref, b_ref, o_ref):
    a = _silu(c_ref[...]).astype(BF16)
    o_ref[...] = jnp.dot(a, w_ref[...].astype(BF16), preferred_element_type=F32) + b_ref[...]


def _adaln(c, w, b):
    m, d = c.shape
    n = w.shape[1]
    tn = _tile(n, 1024)
    return pl.pallas_call(
        _ada_kernel,
        grid=(n // tn,),
        in_specs=[pl.BlockSpec((m, d), lambda j: (0, 0)),
                  pl.BlockSpec((d, tn), lambda j: (0, j)),
                  pl.BlockSpec((1, tn), lambda j: (0, j))],
        out_specs=pl.BlockSpec((m, tn), lambda j: (0, j)),
        out_shape=jax.ShapeDtypeStruct((m, n), F32),
        compiler_params=_cp("arbitrary"),
        name="adaln",
    )(c, w, b.reshape(1, n))


def _modnorm_kernel(x_ref, g_ref, sc_ref, sh_ref, o_ref):
    x = x_ref[...]
    y = x * lax.rsqrt(jnp.mean(x * x, axis=-1, keepdims=True) + EPS) * g_ref[...]
    o_ref[...] = (y * (1.0 + sc_ref[0]) + sh_ref[0]).astype(o_ref.dtype)


def _modnorm(x, g, grp, k_scale, k_shift):
    t, d = x.shape
    tm = grp.row_tile(512)
    return pl.pallas_call(
        _modnorm_kernel,
        grid=(t // tm, 1),
        in_specs=[pl.BlockSpec((tm, d), lambda i, j: (i, 0)),
                  pl.BlockSpec((1, d), lambda i, j: (0, 0)),
                  grp.mod_spec(tm, d), grp.mod_spec(tm, d)],
        out_specs=pl.BlockSpec((tm, d), lambda i, j: (i, 0)),
        out_shape=jax.ShapeDtypeStruct((t, d), BF16),
        compiler_params=_cp("arbitrary", "arbitrary"),
        name="modnorm",
    )(x, g.reshape(1, d), grp.mods[k_scale], grp.mods[k_shift])


def _mm_kernel(a_ref, w_ref, *rest, epi, n_extra):
    extras = rest[:n_extra]
    o_ref = rest[n_extra]
    acc = jnp.dot(a_ref[...], w_ref[...], preferred_element_type=F32)
    vals = [e[0] if len(e.shape) == 3 else e[...] for e in extras]
    o_ref[...] = epi(acc, *vals).astype(o_ref.dtype)


def _mm(a, w, *, tm, tn, out_dtype, epi=None, extras=(), name="mm", w_cols=None):
    t, k = a.shape
    c0, n = (0, w.shape[1]) if w_cols is None else w_cols
    if epi is None:
        epi = lambda acc: acc
    lane = 128
    if c0 % tn != 0 and (c0 % lane != 0 or tn % lane != 0):
        w, c0 = w[:, c0:c0 + n], 0
    if c0 % tn == 0:
        w_spec = pl.BlockSpec((k, tn), lambda i, j: (0, j + c0 // tn))
    else:
        w_spec = pl.BlockSpec((pl.Element(k), pl.Element(tn)),
                              lambda i, j: (0, pl.multiple_of((c0 // lane + j * (tn // lane)) * lane, lane)))
    return pl.pallas_call(
        functools.partial(_mm_kernel, epi=epi, n_extra=len(extras)),
        grid=(t // tm, n // tn),
        in_specs=[pl.BlockSpec((tm, k), lambda i, j: (i, 0)), w_spec] + [s for _, s in extras],
        out_specs=pl.BlockSpec((tm, tn), lambda i, j: (i, j)),
        out_shape=jax.ShapeDtypeStruct((t, n), out_dtype),
        compiler_params=_cp("arbitrary", "arbitrary"),
        name=name,
    )(a, w, *[x for x, _ in extras])


def _tile_spec(tm, tn, col_off):
    assert col_off % tn == 0
    off = col_off // tn
    return pl.BlockSpec((tm, tn), lambda i, j: (i, j + off))


def _row_spec(tn):
    return pl.BlockSpec((1, tn), lambda i, j: (0, j))


def _ssdconv_prompt_kernel(x_ref, hb_ref, w_ref, b_ref, o_ref, u_scr, *, ks):
    seq = x_ref.shape[1]
    u_scr[0:SUBLANES, :] = hb_ref[0]
    u_scr[SUBLANES:, :] = x_ref[0].astype(F32)
    acc = b_ref[...]
    for k in range(ks):
        off = SUBLANES - (ks - 1) + k
        acc = acc + w_ref[k:k + 1, :] * u_scr[off:off + seq, :]
    o_ref[0] = _silu(acc).astype(o_ref.dtype)


def _ssdconv_prompt(proj3, hb, w, b, col_off, out_dtype):
    nb, seq, _ = proj3.shape
    ks, c = w.shape
    tc = _tile(c, 256)
    assert col_off % tc == 0
    off = col_off // tc
    return pl.pallas_call(
        functools.partial(_ssdconv_prompt_kernel, ks=ks),
        grid=(nb, c // tc),
        in_specs=[pl.BlockSpec((1, seq, tc), lambda bi, ci: (bi, 0, ci + off)),
                  pl.BlockSpec((1, SUBLANES, tc), lambda bi, ci: (bi, 0, ci)),
                  pl.BlockSpec((ks, tc), lambda bi, ci: (0, ci)),
                  pl.BlockSpec((1, tc), lambda bi, ci: (0, ci))],
        out_specs=pl.BlockSpec((1, seq, tc), lambda bi, ci: (bi, 0, ci)),
        out_shape=jax.ShapeDtypeStruct((nb, seq, c), out_dtype),
        scratch_shapes=[pltpu.VMEM((seq + SUBLANES, tc), F32)],
        compiler_params=_cp("arbitrary", "arbitrary"),
        name="ssdconv_prompt",
    )(proj3, hb, w, b.reshape(1, c))


def _ssdconv_sample_kernel(x_ref, buf_ref, w_ref, b_ref, o_ref, *, ks):
    steps = x_ref.shape[0]
    slabs = [buf_ref[m] for m in range(ks - 1)] + [x_ref[t].astype(F32) for t in range(steps)]
    for t in range(steps):
        acc = b_ref[...]
        for k in range(ks):
            acc = acc + w_ref[k:k + 1, :] * slabs[t + k]
        o_ref[t] = _silu(acc).astype(o_ref.dtype)


def _ssdconv_sample(proj3, buf_t, w, b, col_off, out_dtype):
    steps, nb, _ = proj3.shape
    ks, c = w.shape
    tc = _tile(c, 512)
    assert col_off % tc == 0
    off = col_off // tc
    return pl.pallas_call(
        functools.partial(_ssdconv_sample_kernel, ks=ks),
        grid=(c // tc,),
        in_specs=[pl.BlockSpec((steps, nb, tc), lambda ci: (0, 0, ci + off)),
                  pl.BlockSpec((ks - 1, nb, tc), lambda ci: (0, 0, ci)),
                  pl.BlockSpec((ks, tc), lambda ci: (0, ci)),
                  pl.BlockSpec((1, tc), lambda ci: (0, ci))],
        out_specs=pl.BlockSpec((steps, nb, tc), lambda ci: (0, 0, ci)),
        out_shape=jax.ShapeDtypeStruct((steps, nb, c), out_dtype),
        compiler_params=_cp("arbitrary"),
        name="ssdconv_sample",
    )(proj3, buf_t, w, b.reshape(1, c))


def _ssd_prompt_kernel(x_ref, b_ref, c_ref, dt_ref, dtT_ref, arow_ref, acol_ref, dsk_ref, s0_ref,
                       y_ref, sfin_ref, s_scr, *, heads, hdim, gb):
    ci = pl.program_id(2)
    q = x_ref.shape[0]
    n_state = b_ref.shape[1] // gb
    rp = heads * hdim

    @pl.when(ci == 0)
    def _():
        s_scr[...] = s0_ref[0]

    ii = lax.broadcasted_iota(jnp.int32, (q, q), 0)
    jj = lax.broadcasted_iota(jnp.int32, (q, q), 1)
    lower = jnp.where(ii >= jj, 1.0, 0.0).astype(BF16)
    upper = jnp.where(ii <= jj, 1.0, 0.0).astype(BF16)
    causal = ii <= jj

    def rows(v):
        return jnp.concatenate(
            [jnp.broadcast_to(v[r:r + 1, :], (hdim, v.shape[1])) for r in range(heads)], axis=0)

    for gi in range(gb):
        ch = slice(gi * rp, (gi + 1) * rp)
        st = slice(gi * n_state, (gi + 1) * n_state)
        dt = dt_ref[gi]
        dt_t = dtT_ref[gi]
        acum = _dot01_left(lower, dt * arow_ref[gi])
        acum_t = _dot01_right(dt_t * acol_ref[gi], upper)
        x_t = x_ref[:, ch].astype(F32).T
        bm = b_ref[:, st]
        cm = c_ref[:, st]
        scores_t = lax.dot_general(bm, cm, NT_DIMS, preferred_element_type=F32)
        last = acum_t[:, q - 1:q]
        xdt_t = (x_t * rows(dt_t)).astype(BF16)
        ydiag = []
        for r in range(heads):
            seg = acum_t[r:r + 1, :] - acum[:, r:r + 1]
            dec = jnp.exp(jnp.where(causal, seg, NEG))
            wgt = (scores_t * dec).astype(BF16)
            ydiag.append(jnp.dot(xdt_t[r * hdim:(r + 1) * hdim, :], wgt, preferred_element_type=F32))
        s_prev = s_scr[ch, :]
        yoff_t = lax.dot_general(s_prev.astype(BF16), cm, NT_DIMS, preferred_element_type=F32)
        y_t = jnp.concatenate(ydiag, axis=0) + yoff_t * rows(jnp.exp(acum_t)) + dsk_ref[gi] * x_t
        y_ref[:, ch] = y_t.T.astype(y_ref.dtype)

        coef_t = dt_t * jnp.exp(last - acum_t)
        chunk_s = jnp.dot((x_t * rows(coef_t)).astype(BF16), bm, preferred_element_type=F32)
        cd = jnp.broadcast_to(jnp.exp(last), (heads, n_state))
        s_scr[ch, :] = rows(cd) * s_prev + chunk_s

    @pl.when(ci == pl.num_programs(2) - 1)
    def _():
        sfin_ref[0] = s_scr[...]


def _ssd_prompt(xbc, dt, a_log, d_skip, s0, nb, seq, groups, heads, hdim, n_state):
    t = xbc.shape[0]
    q = SSD_CHUNK if seq % SSD_CHUNK == 0 else seq
    nc = seq // q
    rp = heads * hdim
    d_inner = groups * rp
    a = -jnp.exp(a_log.astype(F32))
    dtg = dt.reshape(t, groups, heads).transpose(1, 0, 2)
    dtg_t = dt.reshape(t, groups, heads).transpose(1, 2, 0)
    a_row = a.reshape(groups, 1, heads)
    a_col = a.reshape(groups, heads, 1)
    dsk = jnp.broadcast_to(jnp.repeat(d_skip.astype(F32), hdim).reshape(groups, rp, 1), (groups, rp, q))
    gb = next(g for g in (4, 2, 1) if groups % g == 0)
    ng = groups // gb
    gn = gb * n_state
    assert d_inner % gn == 0 and (groups * n_state) % gn == 0
    b_off = d_inner // gn
    c_off = (d_inner + groups * n_state) // gn
    return pl.pallas_call(
        functools.partial(_ssd_prompt_kernel, heads=heads, hdim=hdim, gb=gb),
        grid=(nb, ng, nc),
        in_specs=[pl.BlockSpec((q, gb * rp), lambda b, g, c: (b * nc + c, g)),
                  pl.BlockSpec((q, gn), lambda b, g, c: (b * nc + c, b_off + g)),
                  pl.BlockSpec((q, gn), lambda b, g, c: (b * nc + c, c_off + g)),
                  pl.BlockSpec((gb, q, heads), lambda b, g, c: (g, b * nc + c, 0)),
                  pl.BlockSpec((gb, heads, q), lambda b, g, c: (g, 0, b * nc + c)),
                  pl.BlockSpec((gb, 1, heads), lambda b, g, c: (g, 0, 0)),
                  pl.BlockSpec((gb, heads, 1), lambda b, g, c: (g, 0, 0)),
                  pl.BlockSpec((gb, rp, q), lambda b, g, c: (g, 0, 0)),
                  pl.BlockSpec((1, gb * rp, n_state), lambda b, g, c: (b, g, 0))],
        out_specs=[pl.BlockSpec((q, gb * rp), lambda b, g, c: (b * nc + c, g)),
                   pl.BlockSpec((1, gb * rp, n_state), lambda b, g, c: (b, g, 0))],
        out_shape=[jax.ShapeDtypeStruct((t, d_inner), BF16),
                   jax.ShapeDtypeStruct((nb, groups * rp, n_state), F32)],
        scratch_shapes=[pltpu.VMEM((gb * rp, n_state), F32)],
        compiler_params=_cp("arbitrary", "arbitrary", "arbitrary"),
        name="ssd_prompt",
    )(xbc, xbc, xbc, dtg, dtg_t, a_row, a_col, dsk, s0)


def _ssd_sample_a_kernel(x_ref, b_ref, c_ref, dt_ref, a_ref, dsk_ref, e_ref,
                         yp_ref, ea_ref, xw_ref, cd_ref):
    steps = x_ref.shape[0]
    e01 = e_ref[...]
    a_row = a_ref[0]
    dts = [dt_ref[0, t] for t in range(steps)]
    acum = []
    for t in range(steps):
        da = dts[t] * a_row
        acum.append(da if t == 0 else acum[-1] + da)
    xs = [x_ref[t].astype(F32) for t in range(steps)]
    bs = [b_ref[t].astype(F32) for t in range(steps)]
    cs = [c_ref[t].astype(F32) for t in range(steps)]
    for t in range(steps):
        acc = dsk_ref[0] * xs[t]
        for j in range(t + 1):
            gsc = jnp.sum(cs[t] * bs[j], axis=-1, keepdims=True)
            wgt = gsc * jnp.exp(acum[t] - acum[j]) * dts[j]
            acc = acc + _dot01_right(wgt, e01) * xs[j]
        yp_ref[t] = acc
        ea_ref[t] = _dot01_right(jnp.exp(acum[t]), e01)
        xw_ref[t] = (_dot01_right(dts[t] * jnp.exp(acum[steps - 1] - acum[t]), e01) * xs[t]).astype(xw_ref.dtype)
    cd_ref[0] = jnp.exp(acum[steps - 1])


def _ssd_sample_a(xbc3, dt, a_log, d_skip, groups, heads, hdim, n_state):
    steps, nb, _ = xbc3.shape
    rp = heads * hdim
    d_inner = groups * rp
    a = -jnp.exp(a_log.astype(F32)).reshape(groups, 1, heads)
    dtg = dt.reshape(steps, nb, groups, heads).transpose(2, 0, 1, 3)
    dsk = jnp.repeat(d_skip.astype(F32), hdim).reshape(groups, 1, rp)
    e01 = (jnp.arange(rp)[None, :] // hdim == jnp.arange(heads)[:, None]).astype(BF16)
    b_off = d_inner // n_state
    c_off = (d_inner + groups * n_state) // n_state
    big = pl.BlockSpec((steps, nb, rp), lambda g: (0, 0, g))
    return pl.pallas_call(
        _ssd_sample_a_kernel,
        grid=(groups,),
        in_specs=[big,
                  pl.BlockSpec((steps, nb, n_state), lambda g: (0, 0, b_off + g)),
                  pl.BlockSpec((steps, nb, n_state), lambda g: (0, 0, c_off + g)),
                  pl.BlockSpec((1, steps, nb, heads), lambda g: (g, 0, 0, 0)),
                  pl.BlockSpec((1, 1, heads), lambda g: (g, 0, 0)),
                  pl.BlockSpec((1, 1, rp), lambda g: (g, 0, 0)),
                  pl.BlockSpec((heads, rp), lambda g: (0, 0))],
        out_specs=[big, big, big, pl.BlockSpec((1, nb, heads), lambda g: (g, 0, 0))],
        out_shape=[jax.ShapeDtypeStruct((steps, nb, d_inner), F32),
                   jax.ShapeDtypeStruct((steps, nb, d_inner), F32),
                   jax.ShapeDtypeStruct((steps, nb, d_inner), BF16),
                   jax.ShapeDtypeStruct((groups, nb, heads), F32)],
        compiler_params=_cp("arbitrary"),
        name="ssd_sample_a",
    )(xbc3, xbc3, xbc3, dtg, a, dsk, e01)


def _ssd_sample_b_kernel(s_ref, c_ref, b_ref, xw_ref, ea_ref, yp_ref, cd_ref, y_ref, so_ref,
                         *, groups, heads, hdim):
    rp = heads * hdim
    n_state = s_ref.shape[2]
    rows8 = yp_ref.shape[1]
    cdl = jnp.broadcast_to(cd_ref[0], (groups * heads, n_state))
    for g in range(groups):
        s0 = s_ref[0, g * rp:(g + 1) * rp, :]
        cg = c_ref[0, :, g * n_state:(g + 1) * n_state]
        bg = b_ref[0, :, g * n_state:(g + 1) * n_state]
        yoff = lax.dot_general(cg, s0.astype(BF16), NT_DIMS, preferred_element_type=F32)
        cols = slice(g * rp, (g + 1) * rp)
        y_ref[0, :, cols] = yp_ref[0, :, cols] + ea_ref[0, :, cols] * yoff[:rows8]
        upd = lax.dot_general(xw_ref[0, :, cols], bg, TN_DIMS, preferred_element_type=F32)
        cdrows = jnp.concatenate(
            [jnp.broadcast_to(cdl[g * heads + r:g * heads + r + 1, :], (hdim, n_state))
             for r in range(heads)], axis=0)
        so_ref[0, g * rp:(g + 1) * rp, :] = cdrows * s0 + upd


def _ssd_sample_b(s0, c_b, b_b, xw_b, ea_b, yp_b, cd_col, groups, heads, hdim):
    nb, hp, n_state = s0.shape
    d_inner = hp
    r16 = c_b.shape[1]
    r8 = yp_b.shape[1]
    gn = c_b.shape[2]
    return pl.pallas_call(
        functools.partial(_ssd_sample_b_kernel, groups=groups, heads=heads, hdim=hdim),
        grid=(nb,),
        in_specs=[pl.BlockSpec((1, hp, n_state), lambda b: (b, 0, 0)),
                  pl.BlockSpec((1, r16, gn), lambda b: (b, 0, 0)),
                  pl.BlockSpec((1, r16, gn), lambda b: (b, 0, 0)),
                  pl.BlockSpec((1, r16, d_inner), lambda b: (b, 0, 0)),
                  pl.BlockSpec((1, r8, d_inner), lambda b: (b, 0, 0)),
                  pl.BlockSpec((1, r8, d_inner), lambda b: (b, 0, 0)),
                  pl.BlockSpec((1, groups * heads, 1), lambda b: (b, 0, 0))],
        out_specs=[pl.BlockSpec((1, r8, d_inner), lambda b: (b, 0, 0)),
                   pl.BlockSpec((1, hp, n_state), lambda b: (b, 0, 0))],
        out_shape=[jax.ShapeDtypeStruct((nb, r8, d_inner), F32),
                   jax.ShapeDtypeStruct((nb, hp, n_state), F32)],
        compiler_params=_cp("arbitrary"),
        name="ssd_sample_b",
    )(s0, c_b, b_b, xw_b, ea_b, yp_b, cd_col)


def _gatednorm_kernel(y_ref, z_ref, g_ref, o_ref):
    v = y_ref[...].astype(F32) * _silu(z_ref[...].astype(F32))
    o_ref[...] = (v * lax.rsqrt(jnp.mean(v * v, axis=-1, keepdims=True) + EPS) * g_ref[...]).astype(o_ref.dtype)


def _gatednorm(y, proj, z_off, g, tm):
    t, d = y.shape
    assert z_off % d == 0
    zo = z_off // d
    return pl.pallas_call(
        _gatednorm_kernel,
        grid=(t // tm,),
        in_specs=[pl.BlockSpec((tm, d), lambda i: (i, 0)),
                  pl.BlockSpec((tm, d), lambda i: (i, zo)),
                  pl.BlockSpec((1, d), lambda i: (0, 0))],
        out_specs=pl.BlockSpec((tm, d), lambda i: (i, 0)),
        out_shape=jax.ShapeDtypeStruct((t, d), BF16),
        compiler_params=_cp("arbitrary"),
        name="gatednorm",
    )(y, proj, g.reshape(1, d))


CONF_HALO = 32


def _layernorm_silu(v, g, b):
    mu = jnp.mean(v, axis=-1, keepdims=True)
    vc = v - mu
    var = jnp.mean(vc * vc, axis=-1, keepdims=True)
    return _silu(vc * lax.rsqrt(var + EPS) * g + b)


def _conf_prompt_kernel(ca_ref, cg_ref, hb_ref, w_ref, b_ref, g_ref, lb_ref, o_ref, tail_ref,
                        u_scr, v_scr, sh_scr, *, kc, lane_chunk):
    li = pl.program_id(1)
    tl = ca_ref.shape[1]
    d = ca_ref.shape[2]

    @pl.when(li == 0)
    def _():
        u_scr[0:CONF_HALO, :] = hb_ref[0]

    @pl.when(li > 0)
    def _():
        u_scr[0:CONF_HALO, :] = u_scr[tl:tl + CONF_HALO, :]

    u_scr[CONF_HALO:, :] = ca_ref[0].astype(F32) * jax.nn.sigmoid(cg_ref[0].astype(F32))

    span = tl + CONF_HALO - SUBLANES

    def chunk(ci, carry):
        lanes = pl.ds(pl.multiple_of(ci * lane_chunk, lane_chunk), lane_chunk)
        for r in range(1, SUBLANES):
            sh_scr[r - 1] = u_scr[r:r + span, lanes]
        acc = jnp.broadcast_to(b_ref[:, lanes], (tl, lane_chunk))
        for k in range(kc):
            a, r = divmod(CONF_HALO - (kc - 1) + k, SUBLANES)
            rows = pl.ds(a * SUBLANES, tl)
            src = u_scr[rows, lanes] if r == 0 else sh_scr[r - 1, rows, :]
            acc = acc + w_ref[k:k + 1, lanes] * src
        v_scr[:, lanes] = acc
        return carry

    lax.fori_loop(0, d // lane_chunk, chunk, 0)
    o_ref[...] = _layernorm_silu(v_scr[...], g_ref[...], lb_ref[...]).astype(o_ref.dtype)

    @pl.when(li == pl.num_programs(1) - 1)
    def _():
        tail_ref[0] = u_scr[tl:tl + CONF_HALO, :]


def _conf_prompt(proj3, hb, w, b, ln_g, ln_b, ca_off, cg_off):
    nb, seq, _ = proj3.shape
    kc, d = w.shape
    tl = _tile(seq, 128, CONF_HALO)
    nl = seq // tl
    assert ca_off % d == 0 and cg_off % d == 0 and tl >= CONF_HALO and kc - 1 <= CONF_HALO
    cao, cgo = ca_off // d, cg_off // d
    lane_chunk = _tile(d, 512)
    row = lambda bi, li: (0, 0)
    return pl.pallas_call(
        functools.partial(_conf_prompt_kernel, kc=kc, lane_chunk=lane_chunk),
        grid=(nb, nl),
        in_specs=[pl.BlockSpec((1, tl, d), lambda bi, li: (bi, li, cao)),
                  pl.BlockSpec((1, tl, d), lambda bi, li: (bi, li, cgo)),
                  pl.BlockSpec((1, CONF_HALO, d), lambda bi, li: (bi, 0, 0)),
                  pl.BlockSpec((kc, d), row),
                  pl.BlockSpec((1, d), row), pl.BlockSpec((1, d), row), pl.BlockSpec((1, d), row)],
        out_specs=[pl.BlockSpec((tl, d), lambda bi, li: (bi * nl + li, 0)),
                   pl.BlockSpec((1, CONF_HALO, d), lambda bi, li: (bi, 0, 0))],
        out_shape=[jax.ShapeDtypeStruct((nb * seq, d), BF16),
                   jax.ShapeDtypeStruct((nb, CONF_HALO, d), F32)],
        scratch_shapes=[pltpu.VMEM((tl + CONF_HALO, d), F32), pltpu.VMEM((tl, d), F32),
                        pltpu.VMEM((SUBLANES - 1, tl + CONF_HALO - SUBLANES, lane_chunk), F32)],
        compiler_params=_cp("arbitrary", "arbitrary"),
        name="conf_prompt",
    )(proj3, proj3, hb, w, b.reshape(1, d), ln_g.reshape(1, d), ln_b.reshape(1, d))


def _conf_sample_kernel(ca_ref, cg_ref, buf_ref, w_ref, b_ref, g_ref, lb_ref, o_ref, so_ref,
                        xp_scr, v_scr, *, kc):
    steps, bb, _ = ca_ref.shape
    win = w_ref.shape[0]
    nbuf = kc - 1
    ucs = [ca_ref[t].astype(F32) * jax.nn.sigmoid(cg_ref[t].astype(F32)) for t in range(steps)]
    xp_scr[nbuf + steps:, :] = jnp.zeros((xp_scr.shape[0] - nbuf - steps, xp_scr.shape[1]), F32)
    for b in range(bb):
        xp_scr[0:nbuf, :] = buf_ref[b]
        for t in range(steps):
            xp_scr[nbuf + t:nbuf + t + 1, :] = ucs[t][b:b + 1, :]
        for t in range(steps):
            v_scr[t, b:b + 1, :] = jnp.sum(w_ref[...] * xp_scr[t:t + win, :], axis=0, keepdims=True)
        so_ref[b] = xp_scr[steps:steps + nbuf, :]
    for t in range(steps):
        o_ref[t] = _layernorm_silu(v_scr[t] + b_ref[...], g_ref[...], lb_ref[...]).astype(o_ref.dtype)


def _conf_sample(proj3, buf, w, b, ln_g, ln_b, ca_off, cg_off):
    steps, nb, _ = proj3.shape
    kc, d = w.shape
    assert steps < kc - 1
    bb = SUBLANES
    win = -(-kc // SUBLANES) * SUBLANES
    rows = -(-(steps + win) // SUBLANES) * SUBLANES
    w_pad = jnp.pad(w, ((0, win - kc), (0, 0)))
    cao, cgo = ca_off // d, cg_off // d
    row = lambda i: (0, 0)
    return pl.pallas_call(
        functools.partial(_conf_sample_kernel, kc=kc),
        grid=(nb // bb,),
        in_specs=[pl.BlockSpec((steps, bb, d), lambda i: (0, i, cao)),
                  pl.BlockSpec((steps, bb, d), lambda i: (0, i, cgo)),
                  pl.BlockSpec((bb, kc - 1, d), lambda i: (i, 0, 0)),
                  pl.BlockSpec((win, d), row),
                  pl.BlockSpec((1, d), row), pl.BlockSpec((1, d), row), pl.BlockSpec((1, d), row)],
        out_specs=[pl.BlockSpec((steps, bb, d), lambda i: (0, i, 0)),
                   pl.BlockSpec((bb, kc - 1, d), lambda i: (i, 0, 0))],
        out_shape=[jax.ShapeDtypeStruct((steps, nb, d), BF16),
                   jax.ShapeDtypeStruct((nb, kc - 1, d), F32)],
        scratch_shapes=[pltpu.VMEM((rows, d), F32), pltpu.VMEM((steps, bb, d), F32)],
        compiler_params=_cp("arbitrary"),
        name="conf_sample",
    )(proj3, proj3, buf, w_pad, b.reshape(1, d), ln_g.reshape(1, d), ln_b.reshape(1, d))


def _top_values(s, k, scr):
    cur = s
    for r in range(k):
        m = jnp.max(cur, axis=0, keepdims=True)
        scr[r:r + 1, :] = m
        cur = jnp.where(cur >= m, NEG, cur)
    return scr[...]


def _candidate_sums(v1, v2, k):
    slabs = []
    a = 0
    while a < k and k // (a + 1) > 1:
        cnt = k // (a + 1)
        rows = -(-cnt // SUBLANES) * SUBLANES
        s = v1[a:a + 1, :] + v2[0:rows, :]
        if cnt < rows:
            s = jnp.where(lax.broadcasted_iota(jnp.int32, s.shape, 0) < cnt, s, NEG)
        slabs.append(s)
        a += 1
    if a < k:
        slabs.append(v1[a:k, :] + v2[0:1, :])
    return jnp.concatenate(slabs, axis=0)


def _peer_topk_kernel(q_ref, k1_ref, k2_ref, thr_ref, e1_ref, s2_ref, e2_ref, v1_scr, v2_scr,
                      *, n_heads, half, topk):
    for h in range(n_heads):
        base = h * 2 * half
        q1 = q_ref[:, base:base + half].astype(BF16)
        q2 = q_ref[:, base + half:base + 2 * half].astype(BF16)
        s1 = lax.dot_general(k1_ref[h].astype(BF16), q1, NT_DIMS, preferred_element_type=F32)
        s2 = lax.dot_general(k2_ref[h].astype(BF16), q2, NT_DIMS, preferred_element_type=F32)
        v1 = _top_values(s1, topk, v1_scr)
        v2 = _top_values(s2, topk, v2_scr)
        cand = _candidate_sums(v1, v2, topk)
        cur = cand
        tau = None
        for r in range(topk):
            tau = jnp.max(cur, axis=0, keepdims=True)
            if r < topk - 1:
                cur = jnp.where(cur >= tau, NEG, cur)
        top = v1[0:1, :] + v2[0:1, :]
        z = jnp.sum(jnp.where(cand >= tau, jnp.exp(cand - top), 0.0), axis=0, keepdims=True)
        thr_ref[h] = tau - s1
        e1_ref[h] = jnp.exp(s1 - v1[0:1, :]) / z
        s2_ref[h] = s2
        e2_ref[h] = jnp.exp(s2 - v2[0:1, :])


def _peer_topk(qv, keys1, keys2):
    t, _ = qv.shape
    n_heads, nk, half = keys1.shape
    tq = _tile(t, 256)
    kspec = pl.BlockSpec((n_heads, nk, half), lambda i: (0, 0, 0))
    ospec = pl.BlockSpec((n_heads, nk, tq), lambda i: (0, 0, i))
    oshape = jax.ShapeDtypeStruct((n_heads, nk, t), F32)
    return pl.pallas_call(
        functools.partial(_peer_topk_kernel, n_heads=n_heads, half=half, topk=PEER_TOPK),
        grid=(t // tq,),
        in_specs=[pl.BlockSpec((tq, n_heads * 2 * half), lambda i: (i, 0)), kspec, kspec],
        out_specs=[ospec] * 4,
        out_shape=[oshape] * 4,
        scratch_shapes=[pltpu.VMEM((PEER_TOPK, tq), F32), pltpu.VMEM((PEER_TOPK, tq), F32)],
        compiler_params=_cp("arbitrary"),
        name="peer_topk",
    )(qv, keys1, keys2)


def _gelu_exact(x):
    return 0.5 * x * (1.0 + lax.erf(x * 0.7071067811865476))


def _peer_dense_kernel(h_ref, u_ref, v_ref, thr_ref, e1_ref, s2_ref, e2_ref, o_ref, a_scr, g_scr, act_scr,
                       *, n_heads, nk):
    j = pl.program_id(1)
    n_blocks = pl.num_programs(1) - 1
    te = u_ref.shape[0]
    tm = h_ref.shape[0]
    d = o_ref.shape[1]
    dc = _tile(d, 512)
    tk = _tile(tm, 256)
    lt = _tile(tm, 128)

    def gate_piece(jb, a, c, key_rows):
        if (a, 0) not in key_rows:
            i1 = jb * (te // nk) + a
            for h in range(n_heads):
                key_rows[(a, h)] = (thr_ref[h, pl.ds(i1, 1), :], e1_ref[h, pl.ds(i1, 1), :])
        tok = slice(c * lt, (c + 1) * lt)
        w = jnp.zeros((nk, lt), F32)
        for h in range(n_heads):
            thr, e1 = key_rows[(a, h)]
            w = w + jnp.where(s2_ref[h, :, tok] >= thr[:, tok], e2_ref[h, :, tok] * e1[:, tok], 0.0)
        g_scr[a * nk:(a + 1) * nk, tok] = w

    def step(jb, slot, a_prev):
        pieces = [(a, c) for a in range(te // nk) for c in range(tm // lt)] if jb is not None else []
        n_mm = (tm // tk if jb is not None else 0) + (d // dc if a_prev is not None else 0)
        per = -(-len(pieces) // max(n_mm, 1))
        key_rows = {}

        def some_pieces():
            for _ in range(per):
                if pieces:
                    gate_piece(jb, *pieces.pop(0), key_rows)

        if jb is not None:
            for hf in range(tm // tk):
                tok = slice(hf * tk, (hf + 1) * tk)
                act_scr[:, tok] = _gelu_exact(lax.dot_general(u_ref[...], h_ref[tok, :], NT_DIMS,
                                                              preferred_element_type=F32))
                some_pieces()
        if a_prev is not None:
            for c in range(d // dc):
                cols = slice(c * dc, (c + 1) * dc)
                o_ref[:, cols] += jnp.dot(a_scr[a_prev], v_ref[:, cols], preferred_element_type=F32)
                some_pieces()
        if jb is not None:
            for c in range(tm // lt):
                tok = slice(c * lt, (c + 1) * lt)
                a_scr[slot, tok, :] = (act_scr[:, tok] * g_scr[:, tok]).T.astype(BF16)

    @pl.when(j == 0)
    def _():
        o_ref[...] = jnp.zeros_like(o_ref)
        step(0, 0, None)

    @pl.when(jnp.logical_and(j > 0, j < n_blocks))
    def _():
        slot = lax.rem(j, 2)
        step(j, slot, 1 - slot)

    @pl.when(j == n_blocks)
    def _():
        step(None, None, lax.rem(n_blocks - 1, 2))


def _peer_dense(h2, u, v, thr, e1, s2, e2):
    t, d = h2.shape
    n_exp = u.shape[0]
    n_heads, nk, _ = thr.shape
    tm = _tile(t, 512)
    te = 4 * nk
    n_blocks = n_exp // te
    once = pl.Buffered(1)
    fspec = pl.BlockSpec((n_heads, nk, tm), lambda i, j: (0, 0, i), pipeline_mode=once)
    return pl.pallas_call(
        functools.partial(_peer_dense_kernel, n_heads=n_heads, nk=nk),
        grid=(t // tm, n_blocks + 1),
        in_specs=[pl.BlockSpec((tm, d), lambda i, j: (i, 0), pipeline_mode=once),
                  pl.BlockSpec((te, d), lambda i, j: (jnp.minimum(j, n_blocks - 1), 0)),
                  pl.BlockSpec((te, d), lambda i, j: (jnp.maximum(j - 1, 0), 0)),
                  fspec, fspec, fspec, fspec],
        out_specs=pl.BlockSpec((tm, d), lambda i, j: (i, 0)),
        out_shape=jax.ShapeDtypeStruct((t, d), F32),
        scratch_shapes=[pltpu.VMEM((2, tm, te), BF16), pltpu.VMEM((te, tm), F32), pltpu.VMEM((te, tm), F32)],
        compiler_params=_cp("arbitrary", "arbitrary"),
        name="peer_dense",
    )(h2, u, v, thr, e1, s2, e2)


def _residual_kernel(x_ref, p_ref, gate_ref, g_ref, o_ref, *, final_norm):
    x = x_ref[...] + gate_ref[0] * p_ref[...]
    if final_norm:
        x = x * lax.rsqrt(jnp.mean(x * x, axis=-1, keepdims=True) + EPS) * g_ref[...]
    o_ref[...] = x


def _residual(x1, pe, grp, k_gate, g_final, final_norm):
    t, d = x1.shape
    tm = grp.row_tile(256)
    return pl.pallas_call(
        functools.partial(_residual_kernel, final_norm=final_norm),
        grid=(t // tm, 1),
        in_specs=[pl.BlockSpec((tm, d), lambda i, j: (i, 0)),
                  pl.BlockSpec((tm, d), lambda i, j: (i, 0)),
                  grp.mod_spec(tm, d),
                  pl.BlockSpec((1, d), lambda i, j: (0, 0))],
        out_specs=pl.BlockSpec((tm, d), lambda i, j: (i, 0)),
        out_shape=jax.ShapeDtypeStruct((t, d), F32),
        compiler_params=_cp("arbitrary", "arbitrary"),
        name="residual",
    )(x1, pe, grp.mods[k_gate], g_final.reshape(1, d))


def _softplus(x):
    return jnp.maximum(x, 0.0) + jnp.log1p(jnp.exp(-jnp.abs(x)))


def _layer_group(x, grp, prm, dims, ssd_fn, conv_fn, conf_fn, final_g, final_norm):
    d = dims["d"]
    d_inner = dims["d_inner"]
    off = dims["off"]
    tm = grp.row_tile(1024)
    tn = 1024

    h = _modnorm(x, prm["norm1_g"], grp, 1, 0)
    (a0, na), (b0, nb) = dims["cols_a"], dims["cols_b"]
    proj_a = _mm(h, prm["w_in"], tm=tm, tn=_tile(na, tn), out_dtype=BF16, name="in_proj_a", w_cols=(a0, na))
    proj_b = _mm(h, prm["w_in"], tm=tm, tn=_tile(nb, tn), out_dtype=BF16, name="in_proj_b", w_cols=(b0, nb))
    hs = prm["w_dt"].shape[1]
    dt = _mm(h, prm["w_dt"], tm=tm, tn=hs, out_dtype=F32,
             epi=lambda acc, bias: _softplus(acc + bias),
             extras=[(prm["dt_bias"].reshape(1, hs), _row_spec(hs))], name="dt_proj")

    xbc = conv_fn(proj_a)
    y, ssm_new = ssd_fn(xbc, dt)
    yn = _gatednorm(y, proj_a, off["z"], prm["ssd_norm_g"], grp.row_tile(256))
    tn_d = _tile(d, 512)
    sa = _mm(yn, prm["w_ssd_out"], tm=grp.row_tile(512), tn=tn_d, out_dtype=BF16,
             epi=lambda acc, ga: jax.nn.sigmoid(ga.astype(F32)) * acc,
             extras=[(proj_b, _tile_spec(grp.row_tile(512), tn_d, off["ga"]))], name="ssd_out")

    vc, conf_state = conf_fn(proj_b)
    tmm = grp.row_tile(1024)
    tn_m = _tile(d, 1024)
    mixed = _mm(vc, prm["w_conf_out"], tm=tmm, tn=tn_m, out_dtype=BF16,
                epi=lambda acc, s, gb: s.astype(F32) + jax.nn.sigmoid(gb.astype(F32)) * acc,
                extras=[(sa, _tile_spec(tmm, tn_m, 0)), (proj_b, _tile_spec(tmm, tn_m, off["gb"]))],
                name="conf_out")
    tn_o = _tile(d, 512)
    x1 = _mm(mixed, prm["w_out"], tm=tmm, tn=tn_o, out_dtype=F32,
             epi=lambda acc, xr, gate: xr + gate * acc,
             extras=[(x, _tile_spec(tmm, tn_o, 0)), (grp.mods[2], grp.mod_spec(tmm, tn_o))],
             name="out_proj")

    h2 = _modnorm(x1, prm["norm2_g"], grp, 4, 3)
    qv = _mm(h2, prm["peer_wq"], tm=tmm, tn=_tile(prm["peer_wq"].shape[1], 1024), out_dtype=F32, name="peer_q")
    thr, e1, s2, e2 = _peer_topk(qv, prm["peer_keys1"], prm["peer_keys2"])
    pe = _peer_dense(h2, prm["peer_u"], prm["peer_v"], thr, e1, s2, e2)
    x2 = _residual(x1, pe, grp, 5, final_g, final_norm)
    return x2, proj_a, ssm_new, conf_state


def kernel(x_prompt, x_sample, c_prompt, c_sample, state_ssm, state_ssd_conv, state_conf_conv, w_ada, b_ada, norm1_g, w_in, ssd_conv_w, ssd_conv_b, dt_bias, a_log, d_skip, ssd_norm_g, w_ssd_out, conf_dw_w, conf_dw_b, conf_ln_g, conf_ln_b, w_conf_out, w_out, norm2_g, peer_wq, peer_keys1, peer_keys2, peer_u, peer_v, final_norm_g):
    depth = w_ada.shape[0]
    bp, lp, d = x_prompt.shape
    bs, ls, _ = x_sample.shape
    n_state = state_ssm.shape[-1]
    hdim = state_ssm.shape[-2]
    n_ssm_heads = a_log.shape[-1]
    d_inner = ssd_norm_g.shape[-1]
    conv_dim = ssd_conv_w.shape[-1]
    groups = (conv_dim - d_inner) // (2 * n_state)
    heads = n_ssm_heads // groups
    d_conf = conf_dw_w.shape[-1]
    kc = conf_dw_w.shape[-2]
    ks = ssd_conv_w.shape[-2]
    n_mod = w_ada.shape[-1] // d

    c_xbc, c_dt = d_inner + conv_dim, d_inner + conv_dim + n_ssm_heads
    off = {"z": 0, "xbc": d_inner,
           "ca": 0, "cg": d_conf, "ga": 2 * d_conf, "gb": 2 * d_conf + d}
    ncols_a, ncols_b = c_xbc, 2 * d_conf + 2 * d
    dims = {"d": d, "d_inner": d_inner, "off": off, "cols_a": (0, ncols_a), "cols_b": (c_dt, ncols_b)}

    xp = x_prompt.reshape(bp * lp, d)
    xs = x_sample.transpose(1, 0, 2).reshape(ls * bs, d)
    rows_c = bp + bs
    rows_pad = -(-rows_c // SUBLANES) * SUBLANES
    c_all = jnp.pad(jnp.concatenate([c_prompt, c_sample], axis=0), ((0, rows_pad - rows_c), (0, 0)))

    outs = {k: [] for k in ("ssm_p", "sconv_p", "cconv_p", "ssm_s", "sconv_s", "cconv_s")}
    for li in range(depth):
        final_norm = li == depth - 1
        mod = _adaln(c_all, w_ada[li], b_ada[li])
        mod_p = mod[:bp].reshape(bp, n_mod, d)
        mod_s = mod[bp:bp + bs].reshape(bs, n_mod, d)
        grp_p = _Group(bp * lp, lp, False, [mod_p[:, k][:, None, :] for k in range(n_mod)])
        grp_s = _Group(ls * bs, ls, True, [jnp.tile(mod_s[:, k], (ls, 1))[None] for k in range(n_mod)])

        wi = w_in[li].astype(BF16)
        prm = {
            "norm1_g": norm1_g[li], "norm2_g": norm2_g[li], "ssd_norm_g": ssd_norm_g[li],
            "w_in": wi, "w_dt": wi[:, c_xbc:c_dt], "dt_bias": dt_bias[li].astype(F32),
            "w_ssd_out": w_ssd_out[li].astype(BF16), "w_conf_out": w_conf_out[li].astype(BF16),
            "w_out": w_out[li].astype(BF16), "peer_wq": peer_wq[li].astype(BF16),
            "peer_keys1": peer_keys1[li], "peer_keys2": peer_keys2[li],
            "peer_u": peer_u[li].astype(BF16), "peer_v": peer_v[li].astype(BF16),
        }

        def conv_p(proj):
            hb = jnp.zeros((bp, SUBLANES, conv_dim), F32)
            return _ssdconv_prompt(proj.reshape(bp, lp, ncols_a), hb, ssd_conv_w[li], ssd_conv_b[li],
                                   off["xbc"], BF16).reshape(bp * lp, conv_dim)

        def ssd_p(xbc, dt):
            s0 = jnp.zeros((bp, n_ssm_heads * hdim, n_state), F32)
            return _ssd_prompt(xbc, dt, a_log[li], d_skip[li], s0, bp, lp, groups, heads, hdim, n_state)

        def conf_p(proj):
            hb = jnp.zeros((bp, CONF_HALO, d_conf), F32)
            return _conf_prompt(proj.reshape(bp, lp, ncols_b), hb, conf_dw_w[li], conf_dw_b[li],
                                conf_ln_g[li], conf_ln_b[li], off["ca"], off["cg"])

        xp, proj_p, ssm_p, tail_p = _layer_group(xp, grp_p, prm, dims, ssd_p, conv_p, conf_p,
                                                 final_norm_g, final_norm)
        outs["ssm_p"].append(ssm_p.reshape(bp, n_ssm_heads, hdim, n_state))
        keep_p = min(lp, ks - 1)
        xbc_tail_p = proj_p.reshape(bp, lp, ncols_a)[:, lp - keep_p:, off["xbc"]:].astype(F32)
        sconv0 = jnp.zeros((bp, ks - 1 - keep_p, conv_dim), F32)
        outs["sconv_p"].append(jnp.concatenate([sconv0, xbc_tail_p], axis=1))
        outs["cconv_p"].append(tail_p[:, CONF_HALO - (kc - 1):])

        sbuf = state_ssd_conv[li]
        cbuf = state_conf_conv[li]
        ssm0 = state_ssm[li].reshape(bs, n_ssm_heads * hdim, n_state)

        def conv_s(proj):
            return _ssdconv_sample(proj.reshape(ls, bs, ncols_a), sbuf.transpose(1, 0, 2), ssd_conv_w[li],
                                   ssd_conv_b[li], off["xbc"], BF16).reshape(ls * bs, conv_dim)

        def ssd_s(xbc, dt):
            xbc3 = xbc.reshape(ls, bs, conv_dim)
            yp, ea, xw, cd = _ssd_sample_a(xbc3, dt, a_log[li], d_skip[li], groups, heads, hdim, n_state)
            gn = groups * n_state

            def bmajor(v, rows):
                return jnp.pad(v.transpose(1, 0, 2), ((0, 0), (0, rows - ls), (0, 0)))

            r16 = -(-ls // 16) * 16
            r8 = -(-ls // SUBLANES) * SUBLANES
            b_b = bmajor(xbc3[:, :, d_inner:d_inner + gn], r16)
            c_b = bmajor(xbc3[:, :, d_inner + gn:], r16)
            cd_col = cd.transpose(1, 0, 2).reshape(bs, groups * heads, 1)
            y_b, s_new = _ssd_sample_b(ssm0, c_b, b_b, bmajor(xw, r16), bmajor(ea, r8), bmajor(yp, r8),
                                       cd_col, groups, heads, hdim)
            y = y_b[:, :ls].transpose(1, 0, 2).reshape(ls * bs, d_inner).astype(BF16)
            return y, s_new

        def conf_s(proj):
            vc, cnew = _conf_sample(proj.reshape(ls, bs, ncols_b), cbuf, conf_dw_w[li],
                                    conf_dw_b[li], conf_ln_g[li], conf_ln_b[li], off["ca"], off["cg"])
            return vc.reshape(ls * bs, d_conf), cnew

        xs, proj_s, ssm_s, cconv_s = _layer_group(xs, grp_s, prm, dims, ssd_s, conv_s, conf_s,
                                                  final_norm_g, final_norm)
        outs["ssm_s"].append(ssm_s.reshape(bs, n_ssm_heads, hdim, n_state))
        keep_s = min(ls, ks - 1)
        xbc_tail_s = proj_s.reshape(ls, bs, ncols_a)[ls - keep_s:, :, off["xbc"]:].astype(F32).transpose(1, 0, 2)
        outs["sconv_s"].append(jnp.concatenate([sbuf[:, keep_s:], xbc_tail_s], axis=1))
        outs["cconv_s"].append(cconv_s)

    y_prompt = xp.reshape(bp, lp, d)
    y_sample = xs.reshape(ls, bs, d).transpose(1, 0, 2)
    return (y_prompt, y_sample,
            jnp.stack(outs["ssm_p"], 0), jnp.stack(outs["sconv_p"], 0), jnp.stack(outs["cconv_p"], 0),
            jnp.stack(outs["ssm_s"], 0), jnp.stack(outs["sconv_s"], 0), jnp.stack(outs["cconv_s"], 0))
```

```python
import functools

import jax
import jax.numpy as jnp
from jax import lax
from jax.experimental import pallas as pl
from jax.experimental.pallas import tpu as pltpu

F32 = jnp.float32
BF16 = jnp.bfloat16
EPS = 1e-6
PEER_TOPK = 16
SSD_CHUNK = 128
NEG = -3.0e38
V7X_VMEM_LIMIT_BYTES = 56 * 1024 * 1024
SUBLANES = 8

NT_DIMS = (((1,), (1,)), ((), ()))
TN_DIMS = (((0,), (0,)), ((), ()))


def _cp(*sem):
    return pltpu.CompilerParams(dimension_semantics=sem,
                                vmem_limit_bytes=V7X_VMEM_LIMIT_BYTES)


def _tile(n, pref, mult=128):
    best = None
    t = mult
    while t <= min(n, pref):
        if n % t == 0:
            best = t
        t += mult
    return best if best is not None else n


def _silu(x):
    return x * jax.nn.sigmoid(x)


def _split3(a):
    hi = a.astype(BF16)
    r = a - hi.astype(F32)
    mid = r.astype(BF16)
    lo = (r - mid.astype(F32)).astype(BF16)
    return hi, mid, lo


def _dot01_left(m01, a):
    out = None
    for p in _split3(a):
        t = jnp.dot(m01, p, preferred_element_type=F32)
        out = t if out is None else out + t
    return out


def _dot01_right(a, m01):
    out = None
    for p in _split3(a):
        t = jnp.dot(p, m01, preferred_element_type=F32)
        out = t if out is None else out + t
    return out


class _Group:
    def __init__(self, n_tokens, seq_len, per_token, mods):
        self.T = n_tokens
        self.L = seq_len
        self.per_token = per_token
        self.mods = mods

    def mod_spec(self, tm, tn):
        if self.per_token:
            return pl.BlockSpec((1, tm, tn), lambda i, j: (0, i, j))
        tiles_per_batch = self.L // tm
        return pl.BlockSpec((1, 1, tn), lambda i, j: (i // tiles_per_batch, 0, j))

    def row_tile(self, pref):
        return _tile(self.T if self.per_token else self.L, pref, SUBLANES)


def _ada_kernel(c_ref, w_ref, b_ref, o_ref):
    a = _silu(c_ref[...]).astype(BF16)
    o_ref[...] = jnp.dot(a, w_ref[...].astype(BF16), preferred_element_type=F32) + b_ref[...]


def _adaln(c, w, b):
    m, d = c.shape
    n = w.shape[1]
    tn = _tile(n, 1024)
    return pl.pallas_call(
        _ada_kernel,
        grid=(n // tn,),
        in_specs=[pl.BlockSpec((m, d), lambda j: (0, 0)),
                  pl.BlockSpec((d, tn), lambda j: (0, j)),
                  pl.BlockSpec((1, tn), lambda j: (0, j))],
        out_specs=pl.BlockSpec((m, tn), lambda j: (0, j)),
        out_shape=jax.ShapeDtypeStruct((m, n), F32),
        compiler_params=_cp("arbitrary"),
        name="adaln",
    )(c, w, b.reshape(1, n))


def _modnorm_kernel(x_ref, g_ref, sc_ref, sh_ref, o_ref):
    x = x_ref[...]
    y = x * lax.rsqrt(jnp.mean(x * x, axis=-1, keepdims=True) + EPS) * g_ref[...]
    o_ref[...] = (y * (1.0 + sc_ref[0]) + sh_ref[0]).astype(o_ref.dtype)


def _modnorm(x, g, grp, k_scale, k_shift):
    t, d = x.shape
    tm = grp.row_tile(512)
    return pl.pallas_call(
        _modnorm_kernel,
        grid=(t // tm, 1),
        in_specs=[pl.BlockSpec((tm, d), lambda i, j: (i, 0)),
                  pl.BlockSpec((1, d), lambda i, j: (0, 0)),
                  grp.mod_spec(tm, d), grp.mod_spec(tm, d)],
        out_specs=pl.BlockSpec((tm, d), lambda i, j: (i, 0)),
        out_shape=jax.ShapeDtypeStruct((t, d), BF16),
        compiler_params=_cp("arbitrary", "arbitrary"),
        name="modnorm",
    )(x, g.reshape(1, d), grp.mods[k_scale], grp.mods[k_shift])


def _mm_kernel(a_ref, w_ref, *rest, epi, n_extra, n_side):
    extras = rest[:n_extra]
    side_in = rest[n_extra:n_extra + n_side]
    o_ref = rest[n_extra + n_side]
    side_out = rest[n_extra + n_side + 1:]
    for src, dst in zip(side_in, side_out):
        dst[...] = src[...].astype(dst.dtype)
    acc = jnp.dot(a_ref[...], w_ref[...], preferred_element_type=F32)
    vals = [e[0] if len(e.shape) == 3 else e[...] for e in extras]
    o_ref[...] = epi(acc, *vals).astype(o_ref.dtype)


def _mm(a, w, *, tm, tn, out_dtype, epi=None, extras=(), name="mm", w_cols=None, side=()):
    t, k = a.shape
    c0, n = (0, w.shape[1]) if w_cols is None else w_cols
    if epi is None:
        epi = lambda acc: acc
    grid = (t // tm, n // tn)
    in_specs = [s for _, s in extras]
    operands = [x for x, _ in extras]
    out_specs = [pl.BlockSpec((tm, tn), lambda i, j: (i, j))]
    out_shape = [jax.ShapeDtypeStruct((t, n), out_dtype)]
    for src in side:
        rows = _side_rows(src.shape[0], grid[0] * grid[1])
        last = src.shape[0] // rows - 1
        side_map = lambda i, j, last=last: (jnp.minimum(i * grid[1] + j, last), 0)
        in_specs.append(pl.BlockSpec((rows, src.shape[1]), side_map))
        operands.append(src)
        out_specs.append(pl.BlockSpec((rows, src.shape[1]), side_map))
        out_shape.append(jax.ShapeDtypeStruct(src.shape, BF16))
    lane = 128
    if c0 % tn != 0 and (c0 % lane != 0 or tn % lane != 0):
        w, c0 = w[:, c0:c0 + n], 0
    if c0 % tn == 0:
        w_spec = pl.BlockSpec((k, tn), lambda i, j: (0, j + c0 // tn))
    else:
        w_spec = pl.BlockSpec((pl.Element(k), pl.Element(tn)),
                              lambda i, j: (0, pl.multiple_of((c0 // lane + j * (tn // lane)) * lane, lane)))
    outs = pl.pallas_call(
        functools.partial(_mm_kernel, epi=epi, n_extra=len(extras), n_side=len(side)),
        grid=grid,
        in_specs=[pl.BlockSpec((tm, k), lambda i, j: (i, 0)), w_spec] + in_specs,
        out_specs=out_specs,
        out_shape=out_shape,
        compiler_params=_cp("arbitrary", "arbitrary"),
        name=name,
    )(a, w, *operands)
    return outs[0] if not side else tuple(outs)


def _side_rows(n_rows, steps):
    rows = 16
    while rows < n_rows and (n_rows % rows != 0 or n_rows // rows > steps):
        rows += 16
    return rows


def _tile_spec(tm, tn, col_off):
    assert col_off % tn == 0
    off = col_off // tn
    return pl.BlockSpec((tm, tn), lambda i, j: (i, j + off))


def _row_spec(tn):
    return pl.BlockSpec((1, tn), lambda i, j: (0, j))


def _ssdconv_prompt_kernel(x_ref, hb_ref, w_ref, b_ref, o_ref, u_scr, *, ks):
    seq = x_ref.shape[1]
    u_scr[0:SUBLANES, :] = hb_ref[0]
    u_scr[SUBLANES:, :] = x_ref[0].astype(F32)
    acc = b_ref[...]
    for k in range(ks):
        off = SUBLANES - (ks - 1) + k
        acc = acc + w_ref[k:k + 1, :] * u_scr[off:off + seq, :]
    o_ref[0] = _silu(acc).astype(o_ref.dtype)


def _ssdconv_prompt(proj3, hb, w, b, col_off, out_dtype):
    nb, seq, _ = proj3.shape
    ks, c = w.shape
    tc = _tile(c, 256)
    assert col_off % tc == 0
    off = col_off // tc
    return pl.pallas_call(
        functools.partial(_ssdconv_prompt_kernel, ks=ks),
        grid=(nb, c // tc),
        in_specs=[pl.BlockSpec((1, seq, tc), lambda bi, ci: (bi, 0, ci + off)),
                  pl.BlockSpec((1, SUBLANES, tc), lambda bi, ci: (bi, 0, ci)),
                  pl.BlockSpec((ks, tc), lambda bi, ci: (0, ci)),
                  pl.BlockSpec((1, tc), lambda bi, ci: (0, ci))],
        out_specs=pl.BlockSpec((1, seq, tc), lambda bi, ci: (bi, 0, ci)),
        out_shape=jax.ShapeDtypeStruct((nb, seq, c), out_dtype),
        scratch_shapes=[pltpu.VMEM((seq + SUBLANES, tc), F32)],
        compiler_params=_cp("arbitrary", "arbitrary"),
        name="ssdconv_prompt",
    )(proj3, hb, w, b.reshape(1, c))


def _ssdconv_sample_kernel(x_ref, buf_ref, w_ref, b_ref, o_ref, *, ks):
    steps = x_ref.shape[0]
    slabs = [buf_ref[m] for m in range(ks - 1)] + [x_ref[t].astype(F32) for t in range(steps)]
    for t in range(steps):
        acc = b_ref[...]
        for k in range(ks):
            acc = acc + w_ref[k:k + 1, :] * slabs[t + k]
        o_ref[t] = _silu(acc).astype(o_ref.dtype)


def _ssdconv_sample(proj3, buf_t, w, b, col_off, out_dtype):
    steps, nb, _ = proj3.shape
    ks, c = w.shape
    tc = _tile(c, 512)
    assert col_off % tc == 0
    off = col_off // tc
    return pl.pallas_call(
        functools.partial(_ssdconv_sample_kernel, ks=ks),
        grid=(c // tc,),
        in_specs=[pl.BlockSpec((steps, nb, tc), lambda ci: (0, 0, ci + off)),
                  pl.BlockSpec((ks - 1, nb, tc), lambda ci: (0, 0, ci)),
                  pl.BlockSpec((ks, tc), lambda ci: (0, ci)),
                  pl.BlockSpec((1, tc), lambda ci: (0, ci))],
        out_specs=pl.BlockSpec((steps, nb, tc), lambda ci: (0, 0, ci)),
        out_shape=jax.ShapeDtypeStruct((steps, nb, c), out_dtype),
        compiler_params=_cp("arbitrary"),
        name="ssdconv_sample",
    )(proj3, buf_t, w, b.reshape(1, c))


def _ssd_prompt_kernel(x_ref, b_ref, c_ref, dt_ref, dtT_ref, arow_ref, acol_ref, dsk_ref, s0_ref,
                       y_ref, sfin_ref, s_scr, *, heads, hdim, gb):
    ci = pl.program_id(2)
    q = x_ref.shape[0]
    n_state = b_ref.shape[1] // gb
    rp = heads * hdim

    @pl.when(ci == 0)
    def _():
        s_scr[...] = s0_ref[0]

    ii = lax.broadcasted_iota(jnp.int32, (q, q), 0)
    jj = lax.broadcasted_iota(jnp.int32, (q, q), 1)
    lower = jnp.where(ii >= jj, 1.0, 0.0).astype(BF16)
    upper = jnp.where(ii <= jj, 1.0, 0.0).astype(BF16)
    causal = ii <= jj

    def rows(v):
        return jnp.concatenate(
            [jnp.broadcast_to(v[r:r + 1, :], (hdim, v.shape[1])) for r in range(heads)], axis=0)

    for gi in range(gb):
        ch = slice(gi * rp, (gi + 1) * rp)
        st = slice(gi * n_state, (gi + 1) * n_state)
        dt = dt_ref[gi]
        dt_t = dtT_ref[gi]
        acum = _dot01_left(lower, dt * arow_ref[gi])
        acum_t = _dot01_right(dt_t * acol_ref[gi], upper)
        x_t = x_ref[:, ch].astype(F32).T
        bm = b_ref[:, st]
        cm = c_ref[:, st]
        scores_t = lax.dot_general(bm, cm, NT_DIMS, preferred_element_type=F32)
        last = acum_t[:, q - 1:q]
        xdt_t = (x_t * rows(dt_t)).astype(BF16)
        ydiag = []
        for r in range(heads):
            seg = acum_t[r:r + 1, :] - acum[:, r:r + 1]
            dec = jnp.exp(jnp.where(causal, seg, NEG))
            wgt = (scores_t * dec).astype(BF16)
            ydiag.append(jnp.dot(xdt_t[r * hdim:(r + 1) * hdim, :], wgt, preferred_element_type=F32))
        s_prev = s_scr[ch, :]
        yoff_t = lax.dot_general(s_prev.astype(BF16), cm, NT_DIMS, preferred_element_type=F32)
        y_t = jnp.concatenate(ydiag, axis=0) + yoff_t * rows(jnp.exp(acum_t)) + dsk_ref[gi] * x_t
        y_ref[:, ch] = y_t.T.astype(y_ref.dtype)

        coef_t = dt_t * jnp.exp(last - acum_t)
        chunk_s = jnp.dot((x_t * rows(coef_t)).astype(BF16), bm, preferred_element_type=F32)
        cd = jnp.broadcast_to(jnp.exp(last), (heads, n_state))
        s_scr[ch, :] = rows(cd) * s_prev + chunk_s

    @pl.when(ci == pl.num_programs(2) - 1)
    def _():
        sfin_ref[0] = s_scr[...]


def _ssd_prompt(xbc, dt, a_log, d_skip, s0, nb, seq, groups, heads, hdim, n_state):
    t = xbc.shape[0]
    q = SSD_CHUNK if seq % SSD_CHUNK == 0 else seq
    nc = seq // q
    rp = heads * hdim
    d_inner = groups * rp
    a = -jnp.exp(a_log.astype(F32))
    dtg = dt.reshape(t, groups, heads).transpose(1, 0, 2)
    dtg_t = dt.reshape(t, groups, heads).transpose(1, 2, 0)
    a_row = a.reshape(groups, 1, heads)
    a_col = a.reshape(groups, heads, 1)
    dsk = jnp.broadcast_to(jnp.repeat(d_skip.astype(F32), hdim).reshape(groups, rp, 1), (groups, rp, q))
    gb = next(g for g in (4, 2, 1) if groups % g == 0)
    ng = groups // gb
    gn = gb * n_state
    assert d_inner % gn == 0 and (groups * n_state) % gn == 0
    b_off = d_inner // gn
    c_off = (d_inner + groups * n_state) // gn
    return pl.pallas_call(
        functools.partial(_ssd_prompt_kernel, heads=heads, hdim=hdim, gb=gb),
        grid=(nb, ng, nc),
        in_specs=[pl.BlockSpec((q, gb * rp), lambda b, g, c: (b * nc + c, g)),
                  pl.BlockSpec((q, gn), lambda b, g, c: (b * nc + c, b_off + g)),
                  pl.BlockSpec((q, gn), lambda b, g, c: (b * nc + c, c_off + g)),
                  pl.BlockSpec((gb, q, heads), lambda b, g, c: (g, b * nc + c, 0)),
                  pl.BlockSpec((gb, heads, q), lambda b, g, c: (g, 0, b * nc + c)),
                  pl.BlockSpec((gb, 1, heads), lambda b, g, c: (g, 0, 0)),
                  pl.BlockSpec((gb, heads, 1), lambda b, g, c: (g, 0, 0)),
                  pl.BlockSpec((gb, rp, q), lambda b, g, c: (g, 0, 0)),
                  pl.BlockSpec((1, gb * rp, n_state), lambda b, g, c: (b, g, 0))],
        out_specs=[pl.BlockSpec((q, gb * rp), lambda b, g, c: (b * nc + c, g)),
                   pl.BlockSpec((1, gb * rp, n_state), lambda b, g, c: (b, g, 0))],
        out_shape=[jax.ShapeDtypeStruct((t, d_inner), BF16),
                   jax.ShapeDtypeStruct((nb, groups * rp, n_state), F32)],
        scratch_shapes=[pltpu.VMEM((gb * rp, n_state), F32)],
        compiler_params=_cp("arbitrary", "arbitrary", "arbitrary"),
        name="ssd_prompt",
    )(xbc, xbc, xbc, dtg, dtg_t, a_row, a_col, dsk, s0)


def _ssd_sample_a_kernel(x_ref, b_ref, c_ref, dt_ref, a_ref, dsk_ref, e_ref,
                         yp_ref, ea_ref, xw_ref, cd_ref):
    steps = x_ref.shape[0]
    e01 = e_ref[...]
    a_row = a_ref[0]
    dts = [dt_ref[0, t] for t in range(steps)]
    acum = []
    for t in range(steps):
        da = dts[t] * a_row
        acum.append(da if t == 0 else acum[-1] + da)
    xs = [x_ref[t].astype(F32) for t in range(steps)]
    bs = [b_ref[t].astype(F32) for t in range(steps)]
    cs = [c_ref[t].astype(F32) for t in range(steps)]
    for t in range(steps):
        acc = dsk_ref[0] * xs[t]
        for j in range(t + 1):
            gsc = jnp.sum(cs[t] * bs[j], axis=-1, keepdims=True)
            wgt = gsc * jnp.exp(acum[t] - acum[j]) * dts[j]
            acc = acc + _dot01_right(wgt, e01) * xs[j]
        yp_ref[t] = acc
        ea_ref[t] = _dot01_right(jnp.exp(acum[t]), e01)
        xw_ref[t] = (_dot01_right(dts[t] * jnp.exp(acum[steps - 1] - acum[t]), e01) * xs[t]).astype(xw_ref.dtype)
    cd_ref[0] = jnp.exp(acum[steps - 1])


def _ssd_sample_a(xbc3, dt, a_log, d_skip, groups, heads, hdim, n_state):
    steps, nb, _ = xbc3.shape
    rp = heads * hdim
    d_inner = groups * rp
    a = -jnp.exp(a_log.astype(F32)).reshape(groups, 1, heads)
    dtg = dt.reshape(steps, nb, groups, heads).transpose(2, 0, 1, 3)
    dsk = jnp.repeat(d_skip.astype(F32), hdim).reshape(groups, 1, rp)
    e01 = (jnp.arange(rp)[None, :] // hdim == jnp.arange(heads)[:, None]).astype(BF16)
    b_off = d_inner // n_state
    c_off = (d_inner + groups * n_state) // n_state
    big = pl.BlockSpec((steps, nb, rp), lambda g: (0, 0, g))
    return pl.pallas_call(
        _ssd_sample_a_kernel,
        grid=(groups,),
        in_specs=[big,
                  pl.BlockSpec((steps, nb, n_state), lambda g: (0, 0, b_off + g)),
                  pl.BlockSpec((steps, nb, n_state), lambda g: (0, 0, c_off + g)),
                  pl.BlockSpec((1, steps, nb, heads), lambda g: (g, 0, 0, 0)),
                  pl.BlockSpec((1, 1, heads), lambda g: (g, 0, 0)),
                  pl.BlockSpec((1, 1, rp), lambda g: (g, 0, 0)),
                  pl.BlockSpec((heads, rp), lambda g: (0, 0))],
        out_specs=[big, big, big, pl.BlockSpec((1, nb, heads), lambda g: (g, 0, 0))],
        out_shape=[jax.ShapeDtypeStruct((steps, nb, d_inner), F32),
                   jax.ShapeDtypeStruct((steps, nb, d_inner), F32),
                   jax.ShapeDtypeStruct((steps, nb, d_inner), BF16),
                   jax.ShapeDtypeStruct((groups, nb, heads), F32)],
        compiler_params=_cp("arbitrary"),
        name="ssd_sample_a",
    )(xbc3, xbc3, xbc3, dtg, a, dsk, e01)


def _ssd_sample_b_kernel(s_ref, c_ref, b_ref, xw_ref, ea_ref, yp_ref, cd_ref, y_ref, so_ref,
                         *, groups, heads, hdim):
    rp = heads * hdim
    n_state = s_ref.shape[2]
    rows8 = yp_ref.shape[1]
    cdl = jnp.broadcast_to(cd_ref[0], (groups * heads, n_state))
    for g in range(groups):
        s0 = s_ref[0, g * rp:(g + 1) * rp, :]
        cg = c_ref[0, :, g * n_state:(g + 1) * n_state]
        bg = b_ref[0, :, g * n_state:(g + 1) * n_state]
        yoff = lax.dot_general(cg, s0.astype(BF16), NT_DIMS, preferred_element_type=F32)
        cols = slice(g * rp, (g + 1) * rp)
        y_ref[0, :, cols] = yp_ref[0, :, cols] + ea_ref[0, :, cols] * yoff[:rows8]
        upd = lax.dot_general(xw_ref[0, :, cols], bg, TN_DIMS, preferred_element_type=F32)
        cdrows = jnp.concatenate(
            [jnp.broadcast_to(cdl[g * heads + r:g * heads + r + 1, :], (hdim, n_state))
             for r in range(heads)], axis=0)
        so_ref[0, g * rp:(g + 1) * rp, :] = cdrows * s0 + upd


def _ssd_sample_b(s0, c_b, b_b, xw_b, ea_b, yp_b, cd_col, groups, heads, hdim):
    nb, hp, n_state = s0.shape
    d_inner = hp
    r16 = c_b.shape[1]
    r8 = yp_b.shape[1]
    gn = c_b.shape[2]
    return pl.pallas_call(
        functools.partial(_ssd_sample_b_kernel, groups=groups, heads=heads, hdim=hdim),
        grid=(nb,),
        in_specs=[pl.BlockSpec((1, hp, n_state), lambda b: (b, 0, 0)),
                  pl.BlockSpec((1, r16, gn), lambda b: (b, 0, 0)),
                  pl.BlockSpec((1, r16, gn), lambda b: (b, 0, 0)),
                  pl.BlockSpec((1, r16, d_inner), lambda b: (b, 0, 0)),
                  pl.BlockSpec((1, r8, d_inner), lambda b: (b, 0, 0)),
                  pl.BlockSpec((1, r8, d_inner), lambda b: (b, 0, 0)),
                  pl.BlockSpec((1, groups * heads, 1), lambda b: (b, 0, 0))],
        out_specs=[pl.BlockSpec((1, r8, d_inner), lambda b: (b, 0, 0)),
                   pl.BlockSpec((1, hp, n_state), lambda b: (b, 0, 0))],
        out_shape=[jax.ShapeDtypeStruct((nb, r8, d_inner), F32),
                   jax.ShapeDtypeStruct((nb, hp, n_state), F32)],
        compiler_params=_cp("arbitrary"),
        name="ssd_sample_b",
    )(s0, c_b, b_b, xw_b, ea_b, yp_b, cd_col)


def _gatednorm_kernel(y_ref, z_ref, g_ref, o_ref):
    v = y_ref[...].astype(F32) * _silu(z_ref[...].astype(F32))
    o_ref[...] = (v * lax.rsqrt(jnp.mean(v * v, axis=-1, keepdims=True) + EPS) * g_ref[...]).astype(o_ref.dtype)


def _gatednorm(y, proj, z_off, g, tm):
    t, d = y.shape
    assert z_off % d == 0
    zo = z_off // d
    return pl.pallas_call(
        _gatednorm_kernel,
        grid=(t // tm,),
        in_specs=[pl.BlockSpec((tm, d), lambda i: (i, 0)),
                  pl.BlockSpec((tm, d), lambda i: (i, zo)),
                  pl.BlockSpec((1, d), lambda i: (0, 0))],
        out_specs=pl.BlockSpec((tm, d), lambda i: (i, 0)),
        out_shape=jax.ShapeDtypeStruct((t, d), BF16),
        compiler_params=_cp("arbitrary"),
        name="gatednorm",
    )(y, proj, g.reshape(1, d))


CONF_HALO = 32


def _layernorm_silu(v, g, b):
    mu = jnp.mean(v, axis=-1, keepdims=True)
    vc = v - mu
    var = jnp.mean(vc * vc, axis=-1, keepdims=True)
    return _silu(vc * lax.rsqrt(var + EPS) * g + b)


def _conf_prompt_kernel(ca_ref, cg_ref, hb_ref, w_ref, b_ref, g_ref, lb_ref, o_ref, tail_ref,
                        u_scr, v_scr, sh_scr, *, kc, lane_chunk):
    li = pl.program_id(1)
    tl = ca_ref.shape[1]
    d = ca_ref.shape[2]

    @pl.when(li == 0)
    def _():
        u_scr[0:CONF_HALO, :] = hb_ref[0]

    @pl.when(li > 0)
    def _():
        u_scr[0:CONF_HALO, :] = u_scr[tl:tl + CONF_HALO, :]

    u_scr[CONF_HALO:, :] = ca_ref[0].astype(F32) * jax.nn.sigmoid(cg_ref[0].astype(F32))

    span = tl + CONF_HALO - SUBLANES

    def chunk(ci, carry):
        lanes = pl.ds(pl.multiple_of(ci * lane_chunk, lane_chunk), lane_chunk)
        for r in range(1, SUBLANES):
            sh_scr[r - 1] = u_scr[r:r + span, lanes]
        acc = jnp.broadcast_to(b_ref[:, lanes], (tl, lane_chunk))
        for k in range(kc):
            a, r = divmod(CONF_HALO - (kc - 1) + k, SUBLANES)
            rows = pl.ds(a * SUBLANES, tl)
            src = u_scr[rows, lanes] if r == 0 else sh_scr[r - 1, rows, :]
            acc = acc + w_ref[k:k + 1, lanes] * src
        v_scr[:, lanes] = acc
        return carry

    lax.fori_loop(0, d // lane_chunk, chunk, 0)
    o_ref[...] = _layernorm_silu(v_scr[...], g_ref[...], lb_ref[...]).astype(o_ref.dtype)

    @pl.when(li == pl.num_programs(1) - 1)
    def _():
        tail_ref[0] = u_scr[tl:tl + CONF_HALO, :]


def _conf_prompt(proj3, hb, w, b, ln_g, ln_b, ca_off, cg_off):
    nb, seq, _ = proj3.shape
    kc, d = w.shape
    tl = _tile(seq, 128, CONF_HALO)
    nl = seq // tl
    assert ca_off % d == 0 and cg_off % d == 0 and tl >= CONF_HALO and kc - 1 <= CONF_HALO
    cao, cgo = ca_off // d, cg_off // d
    lane_chunk = _tile(d, 512)
    row = lambda bi, li: (0, 0)
    return pl.pallas_call(
        functools.partial(_conf_prompt_kernel, kc=kc, lane_chunk=lane_chunk),
        grid=(nb, nl),
        in_specs=[pl.BlockSpec((1, tl, d), lambda bi, li: (bi, li, cao)),
                  pl.BlockSpec((1, tl, d), lambda bi, li: (bi, li, cgo)),
                  pl.BlockSpec((1, CONF_HALO, d), lambda bi, li: (bi, 0, 0)),
                  pl.BlockSpec((kc, d), row),
                  pl.BlockSpec((1, d), row), pl.BlockSpec((1, d), row), pl.BlockSpec((1, d), row)],
        out_specs=[pl.BlockSpec((tl, d), lambda bi, li: (bi * nl + li, 0)),
                   pl.BlockSpec((1, CONF_HALO, d), lambda bi, li: (bi, 0, 0))],
        out_shape=[jax.ShapeDtypeStruct((nb * seq, d), BF16),
                   jax.ShapeDtypeStruct((nb, CONF_HALO, d), F32)],
        scratch_shapes=[pltpu.VMEM((tl + CONF_HALO, d), F32), pltpu.VMEM((tl, d), F32),
                        pltpu.VMEM((SUBLANES - 1, tl + CONF_HALO - SUBLANES, lane_chunk), F32)],
        compiler_params=_cp("arbitrary", "arbitrary"),
        name="conf_prompt",
    )(proj3, proj3, hb, w, b.reshape(1, d), ln_g.reshape(1, d), ln_b.reshape(1, d))


def _conf_sample_kernel(ca_ref, cg_ref, buf_ref, w_ref, b_ref, g_ref, lb_ref, o_ref, so_ref,
                        xp_scr, v_scr, *, kc):
    steps, bb, _ = ca_ref.shape
    win = w_ref.shape[0]
    nbuf = kc - 1
    ucs = [ca_ref[t].astype(F32) * jax.nn.sigmoid(cg_ref[t].astype(F32)) for t in range(steps)]
    xp_scr[nbuf + steps:, :] = jnp.zeros((xp_scr.shape[0] - nbuf - steps, xp_scr.shape[1]), F32)
    for b in range(bb):
        xp_scr[0:nbuf, :] = buf_ref[b]
        for t in range(steps):
            xp_scr[nbuf + t:nbuf + t + 1, :] = ucs[t][b:b + 1, :]
        for t in range(steps):
            v_scr[t, b:b + 1, :] = jnp.sum(w_ref[...] * xp_scr[t:t + win, :], axis=0, keepdims=True)
        so_ref[b] = xp_scr[steps:steps + nbuf, :]
    for t in range(steps):
        o_ref[t] = _layernorm_silu(v_scr[t] + b_ref[...], g_ref[...], lb_ref[...]).astype(o_ref.dtype)


def _conf_sample(proj3, buf, w, b, ln_g, ln_b, ca_off, cg_off):
    steps, nb, _ = proj3.shape
    kc, d = w.shape
    assert steps < kc - 1
    bb = SUBLANES
    win = -(-kc // SUBLANES) * SUBLANES
    rows = -(-(steps + win) // SUBLANES) * SUBLANES
    w_pad = jnp.pad(w, ((0, win - kc), (0, 0)))
    cao, cgo = ca_off // d, cg_off // d
    row = lambda i: (0, 0)
    return pl.pallas_call(
        functools.partial(_conf_sample_kernel, kc=kc),
        grid=(nb // bb,),
        in_specs=[pl.BlockSpec((steps, bb, d), lambda i: (0, i, cao)),
                  pl.BlockSpec((steps, bb, d), lambda i: (0, i, cgo)),
                  pl.BlockSpec((bb, kc - 1, d), lambda i: (i, 0, 0)),
                  pl.BlockSpec((win, d), row),
                  pl.BlockSpec((1, d), row), pl.BlockSpec((1, d), row), pl.BlockSpec((1, d), row)],
        out_specs=[pl.BlockSpec((steps, bb, d), lambda i: (0, i, 0)),
                   pl.BlockSpec((bb, kc - 1, d), lambda i: (i, 0, 0))],
        out_shape=[jax.ShapeDtypeStruct((steps, nb, d), BF16),
                   jax.ShapeDtypeStruct((nb, kc - 1, d), F32)],
        scratch_shapes=[pltpu.VMEM((rows, d), F32), pltpu.VMEM((steps, bb, d), F32)],
        compiler_params=_cp("arbitrary"),
        name="conf_sample",
    )(proj3, proj3, buf, w_pad, b.reshape(1, d), ln_g.reshape(1, d), ln_b.reshape(1, d))


def _top_values(s, k, scr):
    cur = s
    for r in range(k):
        m = jnp.max(cur, axis=0, keepdims=True)
        scr[r:r + 1, :] = m
        cur = jnp.where(cur >= m, NEG, cur)
    return scr[...]


def _candidate_sums(v1, v2, k):
    slabs = []
    a = 0
    while a < k and k // (a + 1) > 1:
        cnt = k // (a + 1)
        rows = -(-cnt // SUBLANES) * SUBLANES
        s = v1[a:a + 1, :] + v2[0:rows, :]
        if cnt < rows:
            s = jnp.where(lax.broadcasted_iota(jnp.int32, s.shape, 0) < cnt, s, NEG)
        slabs.append(s)
        a += 1
    if a < k:
        slabs.append(v1[a:k, :] + v2[0:1, :])
    return jnp.concatenate(slabs, axis=0)


def _peer_topk_kernel(q_ref, k1_ref, k2_ref, thr_ref, e1_ref, s2_ref, e2_ref, v1_scr, v2_scr,
                      *, n_heads, half, topk):
    for h in range(n_heads):
        base = h * 2 * half
        q1 = q_ref[:, base:base + half].astype(BF16)
        q2 = q_ref[:, base + half:base + 2 * half].astype(BF16)
        s1 = lax.dot_general(k1_ref[h].astype(BF16), q1, NT_DIMS, preferred_element_type=F32)
        s2 = lax.dot_general(k2_ref[h].astype(BF16), q2, NT_DIMS, preferred_element_type=F32)
        v1 = _top_values(s1, topk, v1_scr)
        v2 = _top_values(s2, topk, v2_scr)
        cand = _candidate_sums(v1, v2, topk)
        cur = cand
        tau = None
        for r in range(topk):
            tau = jnp.max(cur, axis=0, keepdims=True)
            if r < topk - 1:
                cur = jnp.where(cur >= tau, NEG, cur)
        top = v1[0:1, :] + v2[0:1, :]
        z = jnp.sum(jnp.where(cand >= tau, jnp.exp(cand - top), 0.0), axis=0, keepdims=True)
        thr_ref[h] = tau - s1
        e1_ref[h] = jnp.exp(s1 - v1[0:1, :]) / z
        s2_ref[h] = s2
        e2_ref[h] = jnp.exp(s2 - v2[0:1, :])


def _peer_topk(qv, keys1, keys2):
    t, _ = qv.shape
    n_heads, nk, half = keys1.shape
    tq = _tile(t, 256)
    kspec = pl.BlockSpec((n_heads, nk, half), lambda i: (0, 0, 0))
    ospec = pl.BlockSpec((n_heads, nk, tq), lambda i: (0, 0, i))
    oshape = jax.ShapeDtypeStruct((n_heads, nk, t), F32)
    return pl.pallas_call(
        functools.partial(_peer_topk_kernel, n_heads=n_heads, half=half, topk=PEER_TOPK),
        grid=(t // tq,),
        in_specs=[pl.BlockSpec((tq, n_heads * 2 * half), lambda i: (i, 0)), kspec, kspec],
        out_specs=[ospec] * 4,
        out_shape=[oshape] * 4,
        scratch_shapes=[pltpu.VMEM((PEER_TOPK, tq), F32), pltpu.VMEM((PEER_TOPK, tq), F32)],
        compiler_params=_cp("arbitrary"),
        name="peer_topk",
    )(qv, keys1, keys2)


def _gelu_exact(x):
    return 0.5 * x * (1.0 + lax.erf(x * 0.7071067811865476))


def _peer_dense_kernel(h_ref, u_ref, v_ref, thr_ref, e1_ref, s2_ref, e2_ref, o_ref, a_scr, g_scr, act_scr,
                       *, n_heads, nk):
    j = pl.program_id(1)
    n_blocks = pl.num_programs(1) - 1
    te = u_ref.shape[0]
    tm = h_ref.shape[0]
    d = o_ref.shape[1]
    dc = _tile(d, 512)
    tk = _tile(tm, 256)
    lt = _tile(tm, 128)

    def gate_piece(jb, a, c, key_rows):
        if (a, 0) not in key_rows:
            i1 = jb * (te // nk) + a
            for h in range(n_heads):
                key_rows[(a, h)] = (thr_ref[h, pl.ds(i1, 1), :], e1_ref[h, pl.ds(i1, 1), :])
        tok = slice(c * lt, (c + 1) * lt)
        w = jnp.zeros((nk, lt), F32)
        for h in range(n_heads):
            thr, e1 = key_rows[(a, h)]
            w = w + jnp.where(s2_ref[h, :, tok] >= thr[:, tok], e2_ref[h, :, tok] * e1[:, tok], 0.0)
        g_scr[a * nk:(a + 1) * nk, tok] = w

    def step(jb, slot, a_prev):
        pieces = [(a, c) for a in range(te // nk) for c in range(tm // lt)] if jb is not None else []
        n_mm = (tm // tk if jb is not None else 0) + (d // dc if a_prev is not None else 0)
        per = -(-len(pieces) // max(n_mm, 1))
        key_rows = {}

        def some_pieces():
            for _ in range(per):
                if pieces:
                    gate_piece(jb, *pieces.pop(0), key_rows)

        if jb is not None:
            for hf in range(tm // tk):
                tok = slice(hf * tk, (hf + 1) * tk)
                act_scr[:, tok] = _gelu_exact(lax.dot_general(u_ref[...], h_ref[tok, :], NT_DIMS,
                                                              preferred_element_type=F32))
                some_pieces()
        if a_prev is not None:
            for c in range(d // dc):
                cols = slice(c * dc, (c + 1) * dc)
                o_ref[:, cols] += jnp.dot(a_scr[a_prev], v_ref[:, cols], preferred_element_type=F32)
                some_pieces()
        if jb is not None:
            for c in range(tm // lt):
                tok = slice(c * lt, (c + 1) * lt)
                a_scr[slot, tok, :] = (act_scr[:, tok] * g_scr[:, tok]).T.astype(BF16)

    @pl.when(j == 0)
    def _():
        o_ref[...] = jnp.zeros_like(o_ref)
        step(0, 0, None)

    @pl.when(jnp.logical_and(j > 0, j < n_blocks))
    def _():
        slot = lax.rem(j, 2)
        step(j, slot, 1 - slot)

    @pl.when(j == n_blocks)
    def _():
        step(None, None, lax.rem(n_blocks - 1, 2))


def _peer_dense(h2, u, v, thr, e1, s2, e2):
    t, d = h2.shape
    n_exp = u.shape[0]
    n_heads, nk, _ = thr.shape
    tm = _tile(t, 512)
    te = 4 * nk
    n_blocks = n_exp // te
    once = pl.Buffered(1)
    fspec = pl.BlockSpec((n_heads, nk, tm), lambda i, j: (0, 0, i), pipeline_mode=once)
    return pl.pallas_call(
        functools.partial(_peer_dense_kernel, n_heads=n_heads, nk=nk),
        grid=(t // tm, n_blocks + 1),
        in_specs=[pl.BlockSpec((tm, d), lambda i, j: (i, 0), pipeline_mode=once),
                  pl.BlockSpec((te, d), lambda i, j: (jnp.minimum(j, n_blocks - 1), 0)),
                  pl.BlockSpec((te, d), lambda i, j: (jnp.maximum(j - 1, 0), 0)),
                  fspec, fspec, fspec, fspec],
        out_specs=pl.BlockSpec((tm, d), lambda i, j: (i, 0)),
        out_shape=jax.ShapeDtypeStruct((t, d), F32),
        scratch_shapes=[pltpu.VMEM((2, tm, te), BF16), pltpu.VMEM((te, tm), F32), pltpu.VMEM((te, tm), F32)],
        compiler_params=_cp("arbitrary", "arbitrary"),
        name="peer_dense",
    )(h2, u, v, thr, e1, s2, e2)


def _residual_kernel(x_ref, p_ref, gate_ref, g_ref, o_ref, *, final_norm):
    x = x_ref[...] + gate_ref[0] * p_ref[...]
    if final_norm:
        x = x * lax.rsqrt(jnp.mean(x * x, axis=-1, keepdims=True) + EPS) * g_ref[...]
    o_ref[...] = x


def _residual(x1, pe, grp, k_gate, g_final, final_norm):
    t, d = x1.shape
    tm = grp.row_tile(256)
    return pl.pallas_call(
        functools.partial(_residual_kernel, final_norm=final_norm),
        grid=(t // tm, 1),
        in_specs=[pl.BlockSpec((tm, d), lambda i, j: (i, 0)),
                  pl.BlockSpec((tm, d), lambda i, j: (i, 0)),
                  grp.mod_spec(tm, d),
                  pl.BlockSpec((1, d), lambda i, j: (0, 0))],
        out_specs=pl.BlockSpec((tm, d), lambda i, j: (i, 0)),
        out_shape=jax.ShapeDtypeStruct((t, d), F32),
        compiler_params=_cp("arbitrary", "arbitrary"),
        name="residual",
    )(x1, pe, grp.mods[k_gate], g_final.reshape(1, d))


def _softplus(x):
    return jnp.maximum(x, 0.0) + jnp.log1p(jnp.exp(-jnp.abs(x)))


def _layer_group(x, grp, prm, dims, ssd_fn, conv_fn, conf_fn, final_g, final_norm):
    d = dims["d"]
    d_inner = dims["d_inner"]
    off = dims["off"]
    tm = grp.row_tile(1024)
    tn = 1024

    h = _modnorm(x, prm["norm1_g"], grp, 1, 0)
    (a0, na), (b0, nb) = dims["cols_a"], dims["cols_b"]
    tn_a, tn_b = _tile(na, tn), _tile(nb, tn)
    jobs_a = [k for k in ("peer_v", "w_ssd_out", "w_out") if prm[k].dtype != BF16]
    jobs_b = [k for k in ("peer_u", "w_conf_out", "peer_wq") if prm[k].dtype != BF16]
    res_a = _mm(h, prm["w_in"], tm=tm, tn=tn_a, out_dtype=BF16, name="in_proj_a", w_cols=(a0, na),
                side=[prm[k] for k in jobs_a])
    res_b = _mm(h, prm["w_in"], tm=tm, tn=tn_b, out_dtype=BF16, name="in_proj_b", w_cols=(b0, nb),
                side=[prm[k] for k in jobs_b])
    proj_a, proj_b = (res_a[0] if jobs_a else res_a), (res_b[0] if jobs_b else res_b)
    prm = dict(prm, **dict(zip(jobs_a, res_a[1:] if jobs_a else ())), **dict(zip(jobs_b, res_b[1:] if jobs_b else ())))
    hs = prm["w_dt"].shape[1]
    dt = _mm(h, prm["w_dt"], tm=tm, tn=hs, out_dtype=F32,
             epi=lambda acc, bias: _softplus(acc + bias),
             extras=[(prm["dt_bias"].reshape(1, hs), _row_spec(hs))], name="dt_proj")

    xbc = conv_fn(proj_a)
    y, ssm_new = ssd_fn(xbc, dt)
    yn = _gatednorm(y, proj_a, off["z"], prm["ssd_norm_g"], grp.row_tile(256))
    tn_d = _tile(d, 512)
    sa = _mm(yn, prm["w_ssd_out"], tm=grp.row_tile(512), tn=tn_d, out_dtype=BF16,
             epi=lambda acc, ga: jax.nn.sigmoid(ga.astype(F32)) * acc,
             extras=[(proj_b, _tile_spec(grp.row_tile(512), tn_d, off["ga"]))], name="ssd_out")

    vc, conf_state = conf_fn(proj_b)
    tmm = grp.row_tile(1024)
    tn_m = _tile(d, 1024)
    mixed = _mm(vc, prm["w_conf_out"], tm=tmm, tn=tn_m, out_dtype=BF16,
                epi=lambda acc, s, gb: s.astype(F32) + jax.nn.sigmoid(gb.astype(F32)) * acc,
                extras=[(sa, _tile_spec(tmm, tn_m, 0)), (proj_b, _tile_spec(tmm, tn_m, off["gb"]))],
                name="conf_out")
    tn_o = _tile(d, 512)
    x1 = _mm(mixed, prm["w_out"], tm=tmm, tn=tn_o, out_dtype=F32,
             epi=lambda acc, xr, gate: xr + gate * acc,
             extras=[(x, _tile_spec(tmm, tn_o, 0)), (grp.mods[2], grp.mod_spec(tmm, tn_o))],
             name="out_proj")

    h2 = _modnorm(x1, prm["norm2_g"], grp, 4, 3)
    qv = _mm(h2, prm["peer_wq"], tm=tmm, tn=_tile(prm["peer_wq"].shape[1], 1024), out_dtype=F32, name="peer_q")
    thr, e1, s2, e2 = _peer_topk(qv, prm["peer_keys1"], prm["peer_keys2"])
    pe = _peer_dense(h2, prm["peer_u"], prm["peer_v"], thr, e1, s2, e2)
    x2 = _residual(x1, pe, grp, 5, final_g, final_norm)
    return x2, proj_a, ssm_new, conf_state, prm


def kernel(x_prompt, x_sample, c_prompt, c_sample, state_ssm, state_ssd_conv, state_conf_conv, w_ada, b_ada, norm1_g, w_in, ssd_conv_w, ssd_conv_b, dt_bias, a_log, d_skip, ssd_norm_g, w_ssd_out, conf_dw_w, conf_dw_b, conf_ln_g, conf_ln_b, w_conf_out, w_out, norm2_g, peer_wq, peer_keys1, peer_keys2, peer_u, peer_v, final_norm_g):
    depth = w_ada.shape[0]
    bp, lp, d = x_prompt.shape
    bs, ls, _ = x_sample.shape
    n_state = state_ssm.shape[-1]
    hdim = state_ssm.shape[-2]
    n_ssm_heads = a_log.shape[-1]
    d_inner = ssd_norm_g.shape[-1]
    conv_dim = ssd_conv_w.shape[-1]
    groups = (conv_dim - d_inner) // (2 * n_state)
    heads = n_ssm_heads // groups
    d_conf = conf_dw_w.shape[-1]
    kc = conf_dw_w.shape[-2]
    ks = ssd_conv_w.shape[-2]
    n_mod = w_ada.shape[-1] // d

    c_xbc, c_dt = d_inner + conv_dim, d_inner + conv_dim + n_ssm_heads
    off = {"z": 0, "xbc": d_inner,
           "ca": 0, "cg": d_conf, "ga": 2 * d_conf, "gb": 2 * d_conf + d}
    ncols_a, ncols_b = c_xbc, 2 * d_conf + 2 * d
    dims = {"d": d, "d_inner": d_inner, "off": off, "cols_a": (0, ncols_a), "cols_b": (c_dt, ncols_b)}

    xp = x_prompt.reshape(bp * lp, d)
    xs = x_sample.transpose(1, 0, 2).reshape(ls * bs, d)
    rows_c = bp + bs
    rows_pad = -(-rows_c // SUBLANES) * SUBLANES
    c_all = jnp.pad(jnp.concatenate([c_prompt, c_sample], axis=0), ((0, rows_pad - rows_c), (0, 0)))

    outs = {k: [] for k in ("ssm_p", "sconv_p", "cconv_p", "ssm_s", "sconv_s", "cconv_s")}
    for li in range(depth):
        final_norm = li == depth - 1
        mod = _adaln(c_all, w_ada[li], b_ada[li])
        mod_p = mod[:bp].reshape(bp, n_mod, d)
        mod_s = mod[bp:bp + bs].reshape(bs, n_mod, d)
        grp_p = _Group(bp * lp, lp, False, [mod_p[:, k][:, None, :] for k in range(n_mod)])
        grp_s = _Group(ls * bs, ls, True, [jnp.tile(mod_s[:, k], (ls, 1))[None] for k in range(n_mod)])

        wi = w_in[li].astype(BF16)
        prm = {
            "norm1_g": norm1_g[li], "norm2_g": norm2_g[li], "ssd_norm_g": ssd_norm_g[li],
            "w_in": wi, "w_dt": wi[:, c_xbc:c_dt], "dt_bias": dt_bias[li].astype(F32),
            "peer_keys1": peer_keys1[li], "peer_keys2": peer_keys2[li],
            "w_ssd_out": w_ssd_out[li], "w_conf_out": w_conf_out[li], "w_out": w_out[li],
            "peer_wq": peer_wq[li], "peer_u": peer_u[li], "peer_v": peer_v[li],
        }

        def conv_p(proj):
            hb = jnp.zeros((bp, SUBLANES, conv_dim), F32)
            return _ssdconv_prompt(proj.reshape(bp, lp, ncols_a), hb, ssd_conv_w[li], ssd_conv_b[li],
                                   off["xbc"], BF16).reshape(bp * lp, conv_dim)

        def ssd_p(xbc, dt):
            s0 = jnp.zeros((bp, n_ssm_heads * hdim, n_state), F32)
            return _ssd_prompt(xbc, dt, a_log[li], d_skip[li], s0, bp, lp, groups, heads, hdim, n_state)

        def conf_p(proj):
            hb = jnp.zeros((bp, CONF_HALO, d_conf), F32)
            return _conf_prompt(proj.reshape(bp, lp, ncols_b), hb, conf_dw_w[li], conf_dw_b[li],
                                conf_ln_g[li], conf_ln_b[li], off["ca"], off["cg"])

        xp, proj_p, ssm_p, tail_p, prm = _layer_group(xp, grp_p, prm, dims, ssd_p, conv_p, conf_p,
                                                 final_norm_g, final_norm)
        outs["ssm_p"].append(ssm_p.reshape(bp, n_ssm_heads, hdim, n_state))
        keep_p = min(lp, ks - 1)
        xbc_tail_p = proj_p.reshape(bp, lp, ncols_a)[:, lp - keep_p:, off["xbc"]:].astype(F32)
        sconv0 = jnp.zeros((bp, ks - 1 - keep_p, conv_dim), F32)
        outs["sconv_p"].append(jnp.concatenate([sconv0, xbc_tail_p], axis=1))
        outs["cconv_p"].append(tail_p[:, CONF_HALO - (kc - 1):])

        sbuf = state_ssd_conv[li]
        cbuf = state_conf_conv[li]
        ssm0 = state_ssm[li].reshape(bs, n_ssm_heads * hdim, n_state)

        def conv_s(proj):
            return _ssdconv_sample(proj.reshape(ls, bs, ncols_a), sbuf.transpose(1, 0, 2), ssd_conv_w[li],
                                   ssd_conv_b[li], off["xbc"], BF16).reshape(ls * bs, conv_dim)

        def ssd_s(xbc, dt):
            xbc3 = xbc.reshape(ls, bs, conv_dim)
            yp, ea, xw, cd = _ssd_sample_a(xbc3, dt, a_log[li], d_skip[li], groups, heads, hdim, n_state)
            gn = groups * n_state

            def bmajor(v, rows):
                return jnp.pad(v.transpose(1, 0, 2), ((0, 0), (0, rows - ls), (0, 0)))

            r16 = -(-ls // 16) * 16
            r8 = -(-ls // SUBLANES) * SUBLANES
            b_b = bmajor(xbc3[:, :, d_inner:d_inner + gn], r16)
            c_b = bmajor(xbc3[:, :, d_inner + gn:], r16)
            cd_col = cd.transpose(1, 0, 2).reshape(bs, groups * heads, 1)
            y_b, s_new = _ssd_sample_b(ssm0, c_b, b_b, bmajor(xw, r16), bmajor(ea, r8), bmajor(yp, r8),
                                       cd_col, groups, heads, hdim)
            y = y_b[:, :ls].transpose(1, 0, 2).reshape(ls * bs, d_inner).astype(BF16)
            return y, s_new

        def conf_s(proj):
            vc, cnew = _conf_sample(proj.reshape(ls, bs, ncols_b), cbuf, conf_dw_w[li],
                                    conf_dw_b[li], conf_ln_g[li], conf_ln_b[li], off["ca"], off["cg"])
            return vc.reshape(ls * bs, d_conf), cnew

        xs, proj_s, ssm_s, cconv_s, _ = _layer_group(xs, grp_s, prm, dims, ssd_s, conv_s, conf_s,
                                                  final_norm_g, final_norm)
        outs["ssm_s"].append(ssm_s.reshape(bs, n_ssm_heads, hdim, n_state))
        keep_s = min(ls, ks - 1)
        xbc_tail_s = proj_s.reshape(ls, bs, ncols_a)[ls - keep_s:, :, off["xbc"]:].astype(F32).transpose(1, 0, 2)
        outs["sconv_s"].append(jnp.concatenate([sbuf[:, keep_s:], xbc_tail_s], axis=1))
        outs["cconv_s"].append(cconv_s)

    y_prompt = xp.reshape(bp, lp, d)
    y_sample = xs.reshape(ls, bs, d).transpose(1, 0, 2)
    return (y_prompt, y_sample,
            jnp.stack(outs["ssm_p"], 0), jnp.stack(outs["sconv_p"], 0), jnp.stack(outs["cconv_p"], 0),
            jnp.stack(outs["ssm_s"], 0), jnp.stack(outs["sconv_s"], 0), jnp.stack(outs["cconv_s"], 0))
```

```python
import functools

import jax
import jax.numpy as jnp
from jax import lax
from jax.experimental import pallas as pl
from jax.experimental.pallas import tpu as pltpu

F32 = jnp.float32
BF16 = jnp.bfloat16
EPS = 1e-6
PEER_TOPK = 16
SSD_CHUNK = 128
NEG = -3.0e38
V7X_VMEM_LIMIT_BYTES = 56 * 1024 * 1024
SUBLANES = 8

NT_DIMS = (((1,), (1,)), ((), ()))
TN_DIMS = (((0,), (0,)), ((), ()))


def _cp(*sem):
    return pltpu.CompilerParams(dimension_semantics=sem,
                                vmem_limit_bytes=V7X_VMEM_LIMIT_BYTES)


def _tile(n, pref, mult=128):
    best = None
    t = mult
    while t <= min(n, pref):
        if n % t == 0:
            best = t
        t += mult
    return best if best is not None else n


def _silu(x):
    return x * jax.nn.sigmoid(x)


def _split3(a):
    hi = a.astype(BF16)
    r = a - hi.astype(F32)
    mid = r.astype(BF16)
    lo = (r - mid.astype(F32)).astype(BF16)
    return hi, mid, lo


def _dot01_left(m01, a):
    out = None
    for p in _split3(a):
        t = jnp.dot(m01, p, preferred_element_type=F32)
        out = t if out is None else out + t
    return out


def _dot01_right(a, m01):
    out = None
    for p in _split3(a):
        t = jnp.dot(p, m01, preferred_element_type=F32)
        out = t if out is None else out + t
    return out


class _Group:
    def __init__(self, n_tokens, seq_len, per_token, mods):
        self.T = n_tokens
        self.L = seq_len
        self.per_token = per_token
        self.mods = mods

    def mod_spec(self, tm, tn):
        if self.per_token:
            return pl.BlockSpec((1, tm, tn), lambda i, j: (0, i, j))
        tiles_per_batch = self.L // tm
        return pl.BlockSpec((1, 1, tn), lambda i, j: (i // tiles_per_batch, 0, j))

    def row_tile(self, pref):
        return _tile(self.T if self.per_token else self.L, pref, SUBLANES)


def _ada_kernel(c_ref, w_ref, b_ref, o_ref):
    a = _silu(c_ref[...]).astype(BF16)
    o_ref[...] = jnp.dot(a, w_ref[...].astype(BF16), preferred_element_type=F32) + b_ref[...]


def _adaln(c, w, b):
    m, d = c.shape
    n = w.shape[1]
    tn = _tile(n, 1024)
    return pl.pallas_call(
        _ada_kernel,
        grid=(n // tn,),
        in_specs=[pl.BlockSpec((m, d), lambda j: (0, 0)),
                  pl.BlockSpec((d, tn), lambda j: (0, j)),
                  pl.BlockSpec((1, tn), lambda j: (0, j))],
        out_specs=pl.BlockSpec((m, tn), lambda j: (0, j)),
        out_shape=jax.ShapeDtypeStruct((m, n), F32),
        compiler_params=_cp("arbitrary"),
        name="adaln",
    )(c, w, b.reshape(1, n))


def _modnorm_kernel(x_ref, g_ref, sc_ref, sh_ref, o_ref):
    x = x_ref[...]
    y = x * lax.rsqrt(jnp.mean(x * x, axis=-1, keepdims=True) + EPS) * g_ref[...]
    o_ref[...] = (y * (1.0 + sc_ref[0]) + sh_ref[0]).astype(o_ref.dtype)


def _modnorm(x, g, grp, k_scale, k_shift):
    t, d = x.shape
    tm = grp.row_tile(512)
    return pl.pallas_call(
        _modnorm_kernel,
        grid=(t // tm, 1),
        in_specs=[pl.BlockSpec((tm, d), lambda i, j: (i, 0)),
                  pl.BlockSpec((1, d), lambda i, j: (0, 0)),
                  grp.mod_spec(tm, d), grp.mod_spec(tm, d)],
        out_specs=pl.BlockSpec((tm, d), lambda i, j: (i, 0)),
        out_shape=jax.ShapeDtypeStruct((t, d), BF16),
        compiler_params=_cp("arbitrary", "arbitrary"),
        name="modnorm",
    )(x, g.reshape(1, d), grp.mods[k_scale], grp.mods[k_shift])


def _mm_kernel(a_ref, w_ref, *rest, epi, n_extra, n_side):
    extras = rest[:n_extra]
    side_in = rest[n_extra:n_extra + n_side]
    o_ref = rest[n_extra + n_side]
    side_out = rest[n_extra + n_side + 1:]
    for src, dst in zip(side_in, side_out):
        dst[...] = src[...].astype(dst.dtype)
    acc = jnp.dot(a_ref[...], w_ref[...], preferred_element_type=F32)
    vals = [e[0] if len(e.shape) == 3 else e[...] for e in extras]
    o_ref[...] = epi(acc, *vals).astype(o_ref.dtype)


def _mm(a, w, *, tm, tn, out_dtype, epi=None, extras=(), name="mm", w_cols=None, side=()):
    t, k = a.shape
    c0, n = (0, w.shape[1]) if w_cols is None else w_cols
    if epi is None:
        epi = lambda acc: acc
    grid = (t // tm, n // tn)
    in_specs = [s for _, s in extras]
    operands = [x for x, _ in extras]
    out_specs = [pl.BlockSpec((tm, tn), lambda i, j: (i, j))]
    out_shape = [jax.ShapeDtypeStruct((t, n), out_dtype)]
    for job in side:
        src, col0, ncols = job if isinstance(job, tuple) else (job, 0, job.shape[1])
        rows = _side_rows(src.shape[0], grid[0] * grid[1])
        last = src.shape[0] // rows - 1
        side_map = lambda i, j, last=last: (jnp.minimum(i * grid[1] + j, last), 0)
        if ncols == src.shape[1]:
            in_specs.append(pl.BlockSpec((rows, ncols), side_map))
        else:
            in_specs.append(pl.BlockSpec(
                (pl.Element(rows), pl.Element(ncols)),
                lambda i, j, last=last, rows=rows, col0=col0:
                    (pl.multiple_of(jnp.minimum(i * grid[1] + j, last) * rows, rows), col0)))
        operands.append(src)
        out_specs.append(pl.BlockSpec((rows, ncols), side_map))
        out_shape.append(jax.ShapeDtypeStruct((src.shape[0], ncols), BF16))
    lane = 128
    if c0 % tn != 0 and (c0 % lane != 0 or tn % lane != 0):
        w, c0 = w[:, c0:c0 + n], 0
    if c0 % tn == 0:
        w_spec = pl.BlockSpec((k, tn), lambda i, j: (0, j + c0 // tn))
    else:
        w_spec = pl.BlockSpec((pl.Element(k), pl.Element(tn)),
                              lambda i, j: (0, pl.multiple_of((c0 // lane + j * (tn // lane)) * lane, lane)))
    outs = pl.pallas_call(
        functools.partial(_mm_kernel, epi=epi, n_extra=len(extras), n_side=len(side)),
        grid=grid,
        in_specs=[pl.BlockSpec((tm, k), lambda i, j: (i, 0)), w_spec] + in_specs,
        out_specs=out_specs,
        out_shape=out_shape,
        compiler_params=_cp("arbitrary", "arbitrary"),
        name=name,
    )(a, w, *operands)
    return outs[0] if not side else tuple(outs)


def _side_rows(n_rows, steps):
    rows = 16
    while rows < n_rows and (n_rows % rows != 0 or n_rows // rows > steps):
        rows += 16
    return rows


def _tile_spec(tm, tn, col_off):
    assert col_off % tn == 0
    off = col_off // tn
    return pl.BlockSpec((tm, tn), lambda i, j: (i, j + off))


def _row_spec(tn):
    return pl.BlockSpec((1, tn), lambda i, j: (0, j))


def _ssdconv_prompt_kernel(x_ref, hb_ref, w_ref, b_ref, o_ref, u_scr, *, ks):
    seq = x_ref.shape[1]
    u_scr[0:SUBLANES, :] = hb_ref[0]
    u_scr[SUBLANES:, :] = x_ref[0].astype(F32)
    acc = b_ref[...]
    for k in range(ks):
        off = SUBLANES - (ks - 1) + k
        acc = acc + w_ref[k:k + 1, :] * u_scr[off:off + seq, :]
    o_ref[0] = _silu(acc).astype(o_ref.dtype)


def _ssdconv_prompt(proj3, hb, w, b, col_off, out_dtype):
    nb, seq, _ = proj3.shape
    ks, c = w.shape
    tc = _tile(c, 256)
    assert col_off % tc == 0
    off = col_off // tc
    return pl.pallas_call(
        functools.partial(_ssdconv_prompt_kernel, ks=ks),
        grid=(nb, c // tc),
        in_specs=[pl.BlockSpec((1, seq, tc), lambda bi, ci: (bi, 0, ci + off)),
                  pl.BlockSpec((1, SUBLANES, tc), lambda bi, ci: (bi, 0, ci)),
                  pl.BlockSpec((ks, tc), lambda bi, ci: (0, ci)),
                  pl.BlockSpec((1, tc), lambda bi, ci: (0, ci))],
        out_specs=pl.BlockSpec((1, seq, tc), lambda bi, ci: (bi, 0, ci)),
        out_shape=jax.ShapeDtypeStruct((nb, seq, c), out_dtype),
        scratch_shapes=[pltpu.VMEM((seq + SUBLANES, tc), F32)],
        compiler_params=_cp("arbitrary", "arbitrary"),
        name="ssdconv_prompt",
    )(proj3, hb, w, b.reshape(1, c))


def _ssdconv_sample_kernel(x_ref, buf_ref, w_ref, b_ref, o_ref, *, ks):
    steps = x_ref.shape[0]
    slabs = [buf_ref[m] for m in range(ks - 1)] + [x_ref[t].astype(F32) for t in range(steps)]
    for t in range(steps):
        acc = b_ref[...]
        for k in range(ks):
            acc = acc + w_ref[k:k + 1, :] * slabs[t + k]
        o_ref[t] = _silu(acc).astype(o_ref.dtype)


def _ssdconv_sample(proj3, buf_t, w, b, col_off, out_dtype):
    steps, nb, _ = proj3.shape
    ks, c = w.shape
    tc = _tile(c, 512)
    assert col_off % tc == 0
    off = col_off // tc
    return pl.pallas_call(
        functools.partial(_ssdconv_sample_kernel, ks=ks),
        grid=(c // tc,),
        in_specs=[pl.BlockSpec((steps, nb, tc), lambda ci: (0, 0, ci + off)),
                  pl.BlockSpec((ks - 1, nb, tc), lambda ci: (0, 0, ci)),
                  pl.BlockSpec((ks, tc), lambda ci: (0, ci)),
                  pl.BlockSpec((1, tc), lambda ci: (0, ci))],
        out_specs=pl.BlockSpec((steps, nb, tc), lambda ci: (0, 0, ci)),
        out_shape=jax.ShapeDtypeStruct((steps, nb, c), out_dtype),
        compiler_params=_cp("arbitrary"),
        name="ssdconv_sample",
    )(proj3, buf_t, w, b.reshape(1, c))


def _ssd_prompt_kernel(x_ref, b_ref, c_ref, dt_ref, dtT_ref, arow_ref, acol_ref, dsk_ref, s0_ref,
                       y_ref, sfin_ref, s_scr, *, heads, hdim, gb):
    ci = pl.program_id(2)
    q = x_ref.shape[0]
    n_state = b_ref.shape[1] // gb
    rp = heads * hdim

    @pl.when(ci == 0)
    def _():
        s_scr[...] = s0_ref[0]

    ii = lax.broadcasted_iota(jnp.int32, (q, q), 0)
    jj = lax.broadcasted_iota(jnp.int32, (q, q), 1)
    lower = jnp.where(ii >= jj, 1.0, 0.0).astype(BF16)
    upper = jnp.where(ii <= jj, 1.0, 0.0).astype(BF16)
    causal = ii <= jj

    def rows(v):
        return jnp.concatenate(
            [jnp.broadcast_to(v[r:r + 1, :], (hdim, v.shape[1])) for r in range(heads)], axis=0)

    for gi in range(gb):
        ch = slice(gi * rp, (gi + 1) * rp)
        st = slice(gi * n_state, (gi + 1) * n_state)
        dt = dt_ref[gi]
        dt_t = dtT_ref[gi]
        acum = _dot01_left(lower, dt * arow_ref[gi])
        acum_t = _dot01_right(dt_t * acol_ref[gi], upper)
        x_t = x_ref[:, ch].astype(F32).T
        bm = b_ref[:, st]
        cm = c_ref[:, st]
        scores_t = lax.dot_general(bm, cm, NT_DIMS, preferred_element_type=F32)
        last = acum_t[:, q - 1:q]
        xdt_t = (x_t * rows(dt_t)).astype(BF16)
        ydiag = []
        for r in range(heads):
            seg = acum_t[r:r + 1, :] - acum[:, r:r + 1]
            dec = jnp.exp(jnp.where(causal, seg, NEG))
            wgt = (scores_t * dec).astype(BF16)
            ydiag.append(jnp.dot(xdt_t[r * hdim:(r + 1) * hdim, :], wgt, preferred_element_type=F32))
        s_prev = s_scr[ch, :]
        yoff_t = lax.dot_general(s_prev.astype(BF16), cm, NT_DIMS, preferred_element_type=F32)
        y_t = jnp.concatenate(ydiag, axis=0) + yoff_t * rows(jnp.exp(acum_t)) + dsk_ref[gi] * x_t
        y_ref[:, ch] = y_t.T.astype(y_ref.dtype)

        coef_t = dt_t * jnp.exp(last - acum_t)
        chunk_s = jnp.dot((x_t * rows(coef_t)).astype(BF16), bm, preferred_element_type=F32)
        cd = jnp.broadcast_to(jnp.exp(last), (heads, n_state))
        s_scr[ch, :] = rows(cd) * s_prev + chunk_s

    @pl.when(ci == pl.num_programs(2) - 1)
    def _():
        sfin_ref[0] = s_scr[...]


def _ssd_prompt(xbc, dt, a_log, d_skip, s0, nb, seq, groups, heads, hdim, n_state):
    t = xbc.shape[0]
    q = SSD_CHUNK if seq % SSD_CHUNK == 0 else seq
    nc = seq // q
    rp = heads * hdim
    d_inner = groups * rp
    a = -jnp.exp(a_log.astype(F32))
    dtg = dt.reshape(t, groups, heads).transpose(1, 0, 2)
    dtg_t = dt.reshape(t, groups, heads).transpose(1, 2, 0)
    a_row = a.reshape(groups, 1, heads)
    a_col = a.reshape(groups, heads, 1)
    dsk = jnp.broadcast_to(jnp.repeat(d_skip.astype(F32), hdim).reshape(groups, rp, 1), (groups, rp, q))
    gb = next(g for g in (4, 2, 1) if groups % g == 0)
    ng = groups // gb
    gn = gb * n_state
    assert d_inner % gn == 0 and (groups * n_state) % gn == 0
    b_off = d_inner // gn
    c_off = (d_inner + groups * n_state) // gn
    return pl.pallas_call(
        functools.partial(_ssd_prompt_kernel, heads=heads, hdim=hdim, gb=gb),
        grid=(nb, ng, nc),
        in_specs=[pl.BlockSpec((q, gb * rp), lambda b, g, c: (b * nc + c, g)),
                  pl.BlockSpec((q, gn), lambda b, g, c: (b * nc + c, b_off + g)),
                  pl.BlockSpec((q, gn), lambda b, g, c: (b * nc + c, c_off + g)),
                  pl.BlockSpec((gb, q, heads), lambda b, g, c: (g, b * nc + c, 0)),
                  pl.BlockSpec((gb, heads, q), lambda b, g, c: (g, 0, b * nc + c)),
                  pl.BlockSpec((gb, 1, heads), lambda b, g, c: (g, 0, 0)),
                  pl.BlockSpec((gb, heads, 1), lambda b, g, c: (g, 0, 0)),
                  pl.BlockSpec((gb, rp, q), lambda b, g, c: (g, 0, 0)),
                  pl.BlockSpec((1, gb * rp, n_state), lambda b, g, c: (b, g, 0))],
        out_specs=[pl.BlockSpec((q, gb * rp), lambda b, g, c: (b * nc + c, g)),
                   pl.BlockSpec((1, gb * rp, n_state), lambda b, g, c: (b, g, 0))],
        out_shape=[jax.ShapeDtypeStruct((t, d_inner), BF16),
                   jax.ShapeDtypeStruct((nb, groups * rp, n_state), F32)],
        scratch_shapes=[pltpu.VMEM((gb * rp, n_state), F32)],
        compiler_params=_cp("arbitrary", "arbitrary", "arbitrary"),
        name="ssd_prompt",
    )(xbc, xbc, xbc, dtg, dtg_t, a_row, a_col, dsk, s0)


def _ssd_sample_a_kernel(x_ref, b_ref, c_ref, dt_ref, a_ref, dsk_ref, e_ref,
                         yp_ref, ea_ref, xw_ref, cd_ref):
    steps = x_ref.shape[0]
    e01 = e_ref[...]
    a_row = a_ref[0]
    dts = [dt_ref[0, t] for t in range(steps)]
    acum = []
    for t in range(steps):
        da = dts[t] * a_row
        acum.append(da if t == 0 else acum[-1] + da)
    xs = [x_ref[t].astype(F32) for t in range(steps)]
    bs = [b_ref[t].astype(F32) for t in range(steps)]
    cs = [c_ref[t].astype(F32) for t in range(steps)]
    for t in range(steps):
        acc = dsk_ref[0] * xs[t]
        for j in range(t + 1):
            gsc = jnp.sum(cs[t] * bs[j], axis=-1, keepdims=True)
            wgt = gsc * jnp.exp(acum[t] - acum[j]) * dts[j]
            acc = acc + _dot01_right(wgt, e01) * xs[j]
        yp_ref[t] = acc
        ea_ref[t] = _dot01_right(jnp.exp(acum[t]), e01)
        xw_ref[t] = (_dot01_right(dts[t] * jnp.exp(acum[steps - 1] - acum[t]), e01) * xs[t]).astype(xw_ref.dtype)
    cd_ref[0] = jnp.exp(acum[steps - 1])


def _ssd_sample_a(xbc3, dt, a_log, d_skip, groups, heads, hdim, n_state):
    steps, nb, _ = xbc3.shape
    rp = heads * hdim
    d_inner = groups * rp
    a = -jnp.exp(a_log.astype(F32)).reshape(groups, 1, heads)
    dtg = dt.reshape(steps, nb, groups, heads).transpose(2, 0, 1, 3)
    dsk = jnp.repeat(d_skip.astype(F32), hdim).reshape(groups, 1, rp)
    e01 = (jnp.arange(rp)[None, :] // hdim == jnp.arange(heads)[:, None]).astype(BF16)
    b_off = d_inner // n_state
    c_off = (d_inner + groups * n_state) // n_state
    big = pl.BlockSpec((steps, nb, rp), lambda g: (0, 0, g))
    return pl.pallas_call(
        _ssd_sample_a_kernel,
        grid=(groups,),
        in_specs=[big,
                  pl.BlockSpec((steps, nb, n_state), lambda g: (0, 0, b_off + g)),
                  pl.BlockSpec((steps, nb, n_state), lambda g: (0, 0, c_off + g)),
                  pl.BlockSpec((1, steps, nb, heads), lambda g: (g, 0, 0, 0)),
                  pl.BlockSpec((1, 1, heads), lambda g: (g, 0, 0)),
                  pl.BlockSpec((1, 1, rp), lambda g: (g, 0, 0)),
                  pl.BlockSpec((heads, rp), lambda g: (0, 0))],
        out_specs=[big, big, big, pl.BlockSpec((1, nb, heads), lambda g: (g, 0, 0))],
        out_shape=[jax.ShapeDtypeStruct((steps, nb, d_inner), F32),
                   jax.ShapeDtypeStruct((steps, nb, d_inner), F32),
                   jax.ShapeDtypeStruct((steps, nb, d_inner), BF16),
                   jax.ShapeDtypeStruct((groups, nb, heads), F32)],
        compiler_params=_cp("arbitrary"),
        name="ssd_sample_a",
    )(xbc3, xbc3, xbc3, dtg, a, dsk, e01)


def _ssd_sample_b_kernel(s_ref, c_ref, b_ref, xw_ref, ea_ref, yp_ref, cd_ref, y_ref, so_ref,
                         *, groups, heads, hdim):
    rp = heads * hdim
    n_state = s_ref.shape[2]
    rows8 = yp_ref.shape[1]
    cdl = jnp.broadcast_to(cd_ref[0], (groups * heads, n_state))
    for g in range(groups):
        s0 = s_ref[0, g * rp:(g + 1) * rp, :]
        cg = c_ref[0, :, g * n_state:(g + 1) * n_state]
        bg = b_ref[0, :, g * n_state:(g + 1) * n_state]
        yoff = lax.dot_general(cg, s0.astype(BF16), NT_DIMS, preferred_element_type=F32)
        cols = slice(g * rp, (g + 1) * rp)
        y_ref[0, :, cols] = yp_ref[0, :, cols] + ea_ref[0, :, cols] * yoff[:rows8]
        upd = lax.dot_general(xw_ref[0, :, cols], bg, TN_DIMS, preferred_element_type=F32)
        cdrows = jnp.concatenate(
            [jnp.broadcast_to(cdl[g * heads + r:g * heads + r + 1, :], (hdim, n_state))
             for r in range(heads)], axis=0)
        so_ref[0, g * rp:(g + 1) * rp, :] = cdrows * s0 + upd


def _ssd_sample_b(s0, c_b, b_b, xw_b, ea_b, yp_b, cd_col, groups, heads, hdim):
    nb, hp, n_state = s0.shape
    d_inner = hp
    r16 = c_b.shape[1]
    r8 = yp_b.shape[1]
    gn = c_b.shape[2]
    return pl.pallas_call(
        functools.partial(_ssd_sample_b_kernel, groups=groups, heads=heads, hdim=hdim),
        grid=(nb,),
        in_specs=[pl.BlockSpec((1, hp, n_state), lambda b: (b, 0, 0)),
                  pl.BlockSpec((1, r16, gn), lambda b: (b, 0, 0)),
                  pl.BlockSpec((1, r16, gn), lambda b: (b, 0, 0)),
                  pl.BlockSpec((1, r16, d_inner), lambda b: (b, 0, 0)),
                  pl.BlockSpec((1, r8, d_inner), lambda b: (b, 0, 0)),
                  pl.BlockSpec((1, r8, d_inner), lambda b: (b, 0, 0)),
                  pl.BlockSpec((1, groups * heads, 1), lambda b: (b, 0, 0))],
        out_specs=[pl.BlockSpec((1, r8, d_inner), lambda b: (b, 0, 0)),
                   pl.BlockSpec((1, hp, n_state), lambda b: (b, 0, 0))],
        out_shape=[jax.ShapeDtypeStruct((nb, r8, d_inner), F32),
                   jax.ShapeDtypeStruct((nb, hp, n_state), F32)],
        compiler_params=_cp("arbitrary"),
        name="ssd_sample_b",
    )(s0, c_b, b_b, xw_b, ea_b, yp_b, cd_col)


def _gatednorm_kernel(y_ref, z_ref, g_ref, o_ref):
    v = y_ref[...].astype(F32) * _silu(z_ref[...].astype(F32))
    o_ref[...] = (v * lax.rsqrt(jnp.mean(v * v, axis=-1, keepdims=True) + EPS) * g_ref[...]).astype(o_ref.dtype)


def _gatednorm(y, proj, z_off, g, tm):
    t, d = y.shape
    assert z_off % d == 0
    zo = z_off // d
    return pl.pallas_call(
        _gatednorm_kernel,
        grid=(t // tm,),
        in_specs=[pl.BlockSpec((tm, d), lambda i: (i, 0)),
                  pl.BlockSpec((tm, d), lambda i: (i, zo)),
                  pl.BlockSpec((1, d), lambda i: (0, 0))],
        out_specs=pl.BlockSpec((tm, d), lambda i: (i, 0)),
        out_shape=jax.ShapeDtypeStruct((t, d), BF16),
        compiler_params=_cp("arbitrary"),
        name="gatednorm",
    )(y, proj, g.reshape(1, d))


CONF_HALO = 32


def _layernorm_silu(v, g, b):
    mu = jnp.mean(v, axis=-1, keepdims=True)
    vc = v - mu
    var = jnp.mean(vc * vc, axis=-1, keepdims=True)
    return _silu(vc * lax.rsqrt(var + EPS) * g + b)


def _conf_prompt_kernel(ca_ref, cg_ref, hb_ref, w_ref, b_ref, g_ref, lb_ref, o_ref, tail_ref,
                        u_scr, v_scr, sh_scr, *, kc, lane_chunk):
    li = pl.program_id(1)
    tl = ca_ref.shape[1]
    d = ca_ref.shape[2]

    @pl.when(li == 0)
    def _():
        u_scr[0:CONF_HALO, :] = hb_ref[0]

    @pl.when(li > 0)
    def _():
        u_scr[0:CONF_HALO, :] = u_scr[tl:tl + CONF_HALO, :]

    u_scr[CONF_HALO:, :] = ca_ref[0].astype(F32) * jax.nn.sigmoid(cg_ref[0].astype(F32))

    span = tl + CONF_HALO - SUBLANES

    def chunk(ci, carry):
        lanes = pl.ds(pl.multiple_of(ci * lane_chunk, lane_chunk), lane_chunk)
        for r in range(1, SUBLANES):
            sh_scr[r - 1] = u_scr[r:r + span, lanes]
        acc = jnp.broadcast_to(b_ref[:, lanes], (tl, lane_chunk))
        for k in range(kc):
            a, r = divmod(CONF_HALO - (kc - 1) + k, SUBLANES)
            rows = pl.ds(a * SUBLANES, tl)
            src = u_scr[rows, lanes] if r == 0 else sh_scr[r - 1, rows, :]
            acc = acc + w_ref[k:k + 1, lanes] * src
        v_scr[:, lanes] = acc
        return carry

    lax.fori_loop(0, d // lane_chunk, chunk, 0)
    o_ref[...] = _layernorm_silu(v_scr[...], g_ref[...], lb_ref[...]).astype(o_ref.dtype)

    @pl.when(li == pl.num_programs(1) - 1)
    def _():
        tail_ref[0] = u_scr[tl:tl + CONF_HALO, :]


def _conf_prompt(proj3, hb, w, b, ln_g, ln_b, ca_off, cg_off):
    nb, seq, _ = proj3.shape
    kc, d = w.shape
    tl = _tile(seq, 128, CONF_HALO)
    nl = seq // tl
    assert ca_off % d == 0 and cg_off % d == 0 and tl >= CONF_HALO and kc - 1 <= CONF_HALO
    cao, cgo = ca_off // d, cg_off // d
    lane_chunk = _tile(d, 512)
    row = lambda bi, li: (0, 0)
    return pl.pallas_call(
        functools.partial(_conf_prompt_kernel, kc=kc, lane_chunk=lane_chunk),
        grid=(nb, nl),
        in_specs=[pl.BlockSpec((1, tl, d), lambda bi, li: (bi, li, cao)),
                  pl.BlockSpec((1, tl, d), lambda bi, li: (bi, li, cgo)),
                  pl.BlockSpec((1, CONF_HALO, d), lambda bi, li: (bi, 0, 0)),
                  pl.BlockSpec((kc, d), row),
                  pl.BlockSpec((1, d), row), pl.BlockSpec((1, d), row), pl.BlockSpec((1, d), row)],
        out_specs=[pl.BlockSpec((tl, d), lambda bi, li: (bi * nl + li, 0)),
                   pl.BlockSpec((1, CONF_HALO, d), lambda bi, li: (bi, 0, 0))],
        out_shape=[jax.ShapeDtypeStruct((nb * seq, d), BF16),
                   jax.ShapeDtypeStruct((nb, CONF_HALO, d), F32)],
        scratch_shapes=[pltpu.VMEM((tl + CONF_HALO, d), F32), pltpu.VMEM((tl, d), F32),
                        pltpu.VMEM((SUBLANES - 1, tl + CONF_HALO - SUBLANES, lane_chunk), F32)],
        compiler_params=_cp("arbitrary", "arbitrary"),
        name="conf_prompt",
    )(proj3, proj3, hb, w, b.reshape(1, d), ln_g.reshape(1, d), ln_b.reshape(1, d))


def _conf_sample_kernel(ca_ref, cg_ref, buf_ref, w_ref, b_ref, g_ref, lb_ref, o_ref, so_ref,
                        xp_scr, v_scr, *, kc):
    steps, bb, _ = ca_ref.shape
    win = w_ref.shape[0]
    nbuf = kc - 1
    ucs = [ca_ref[t].astype(F32) * jax.nn.sigmoid(cg_ref[t].astype(F32)) for t in range(steps)]
    xp_scr[nbuf + steps:, :] = jnp.zeros((xp_scr.shape[0] - nbuf - steps, xp_scr.shape[1]), F32)
    for b in range(bb):
        xp_scr[0:nbuf, :] = buf_ref[b]
        for t in range(steps):
            xp_scr[nbuf + t:nbuf + t + 1, :] = ucs[t][b:b + 1, :]
        for t in range(steps):
            v_scr[t, b:b + 1, :] = jnp.sum(w_ref[...] * xp_scr[t:t + win, :], axis=0, keepdims=True)
        so_ref[b] = xp_scr[steps:steps + nbuf, :]
    for t in range(steps):
        o_ref[t] = _layernorm_silu(v_scr[t] + b_ref[...], g_ref[...], lb_ref[...]).astype(o_ref.dtype)


def _conf_sample(proj3, buf, w, b, ln_g, ln_b, ca_off, cg_off):
    steps, nb, _ = proj3.shape
    kc, d = w.shape
    assert steps < kc - 1
    bb = SUBLANES
    win = -(-kc // SUBLANES) * SUBLANES
    rows = -(-(steps + win) // SUBLANES) * SUBLANES
    w_pad = jnp.pad(w, ((0, win - kc), (0, 0)))
    cao, cgo = ca_off // d, cg_off // d
    row = lambda i: (0, 0)
    return pl.pallas_call(
        functools.partial(_conf_sample_kernel, kc=kc),
        grid=(nb // bb,),
        in_specs=[pl.BlockSpec((steps, bb, d), lambda i: (0, i, cao)),
                  pl.BlockSpec((steps, bb, d), lambda i: (0, i, cgo)),
                  pl.BlockSpec((bb, kc - 1, d), lambda i: (i, 0, 0)),
                  pl.BlockSpec((win, d), row),
                  pl.BlockSpec((1, d), row), pl.BlockSpec((1, d), row), pl.BlockSpec((1, d), row)],
        out_specs=[pl.BlockSpec((steps, bb, d), lambda i: (0, i, 0)),
                   pl.BlockSpec((bb, kc - 1, d), lambda i: (i, 0, 0))],
        out_shape=[jax.ShapeDtypeStruct((steps, nb, d), BF16),
                   jax.ShapeDtypeStruct((nb, kc - 1, d), F32)],
        scratch_shapes=[pltpu.VMEM((rows, d), F32), pltpu.VMEM((steps, bb, d), F32)],
        compiler_params=_cp("arbitrary"),
        name="conf_sample",
    )(proj3, proj3, buf, w_pad, b.reshape(1, d), ln_g.reshape(1, d), ln_b.reshape(1, d))


def _top_values(s, k, scr):
    cur = s
    for r in range(k):
        m = jnp.max(cur, axis=0, keepdims=True)
        scr[r:r + 1, :] = m
        cur = jnp.where(cur >= m, NEG, cur)
    return scr[...]


def _candidate_sums(v1, v2, k):
    slabs = []
    a = 0
    while a < k and k // (a + 1) > 1:
        cnt = k // (a + 1)
        rows = -(-cnt // SUBLANES) * SUBLANES
        s = v1[a:a + 1, :] + v2[0:rows, :]
        if cnt < rows:
            s = jnp.where(lax.broadcasted_iota(jnp.int32, s.shape, 0) < cnt, s, NEG)
        slabs.append(s)
        a += 1
    if a < k:
        slabs.append(v1[a:k, :] + v2[0:1, :])
    return jnp.concatenate(slabs, axis=0)


def _peer_topk_kernel(q_ref, k1_ref, k2_ref, thr_ref, e1_ref, s2_ref, e2_ref, v1_scr, v2_scr,
                      *, n_heads, half, topk):
    for h in range(n_heads):
        base = h * 2 * half
        q1 = q_ref[:, base:base + half].astype(BF16)
        q2 = q_ref[:, base + half:base + 2 * half].astype(BF16)
        s1 = lax.dot_general(k1_ref[h].astype(BF16), q1, NT_DIMS, preferred_element_type=F32)
        s2 = lax.dot_general(k2_ref[h].astype(BF16), q2, NT_DIMS, preferred_element_type=F32)
        v1 = _top_values(s1, topk, v1_scr)
        v2 = _top_values(s2, topk, v2_scr)
        cand = _candidate_sums(v1, v2, topk)
        cur = cand
        tau = None
        for r in range(topk):
            tau = jnp.max(cur, axis=0, keepdims=True)
            if r < topk - 1:
                cur = jnp.where(cur >= tau, NEG, cur)
        top = v1[0:1, :] + v2[0:1, :]
        z = jnp.sum(jnp.where(cand >= tau, jnp.exp(cand - top), 0.0), axis=0, keepdims=True)
        thr_ref[h] = tau - s1
        e1_ref[h] = jnp.exp(s1 - v1[0:1, :]) / z
        s2_ref[h] = s2
        e2_ref[h] = jnp.exp(s2 - v2[0:1, :])


def _peer_topk(qv, keys1, keys2):
    t, _ = qv.shape
    n_heads, nk, half = keys1.shape
    tq = _tile(t, 256)
    kspec = pl.BlockSpec((n_heads, nk, half), lambda i: (0, 0, 0))
    ospec = pl.BlockSpec((n_heads, nk, tq), lambda i: (0, 0, i))
    oshape = jax.ShapeDtypeStruct((n_heads, nk, t), F32)
    return pl.pallas_call(
        functools.partial(_peer_topk_kernel, n_heads=n_heads, half=half, topk=PEER_TOPK),
        grid=(t // tq,),
        in_specs=[pl.BlockSpec((tq, n_heads * 2 * half), lambda i: (i, 0)), kspec, kspec],
        out_specs=[ospec] * 4,
        out_shape=[oshape] * 4,
        scratch_shapes=[pltpu.VMEM((PEER_TOPK, tq), F32), pltpu.VMEM((PEER_TOPK, tq), F32)],
        compiler_params=_cp("arbitrary"),
        name="peer_topk",
    )(qv, keys1, keys2)


def _gelu_exact(x):
    return 0.5 * x * (1.0 + lax.erf(x * 0.7071067811865476))


def _peer_dense_kernel(h_ref, u_ref, v_ref, thr_ref, e1_ref, s2_ref, e2_ref, o_ref, a_scr, g_scr, act_scr,
                       *, n_heads, nk):
    j = pl.program_id(1)
    n_blocks = pl.num_programs(1) - 1
    te = u_ref.shape[0]
    tm = h_ref.shape[0]
    d = o_ref.shape[1]
    dc = _tile(d, 512)
    tk = _tile(tm, 256)
    lt = _tile(tm, 128)

    def gate_piece(jb, a, c, key_rows):
        if (a, 0) not in key_rows:
            i1 = jb * (te // nk) + a
            for h in range(n_heads):
                key_rows[(a, h)] = (thr_ref[h, pl.ds(i1, 1), :], e1_ref[h, pl.ds(i1, 1), :])
        tok = slice(c * lt, (c + 1) * lt)
        w = jnp.zeros((nk, lt), F32)
        for h in range(n_heads):
            thr, e1 = key_rows[(a, h)]
            w = w + jnp.where(s2_ref[h, :, tok] >= thr[:, tok], e2_ref[h, :, tok] * e1[:, tok], 0.0)
        g_scr[a * nk:(a + 1) * nk, tok] = w

    def step(jb, slot, a_prev):
        pieces = [(a, c) for a in range(te // nk) for c in range(tm // lt)] if jb is not None else []
        n_mm = (tm // tk if jb is not None else 0) + (d // dc if a_prev is not None else 0)
        per = -(-len(pieces) // max(n_mm, 1))
        key_rows = {}

        def some_pieces():
            for _ in range(per):
                if pieces:
                    gate_piece(jb, *pieces.pop(0), key_rows)

        if jb is not None:
            for hf in range(tm // tk):
                tok = slice(hf * tk, (hf + 1) * tk)
                act_scr[:, tok] = _gelu_exact(lax.dot_general(u_ref[...], h_ref[tok, :], NT_DIMS,
                                                              preferred_element_type=F32))
                some_pieces()
        if a_prev is not None:
            for c in range(d // dc):
                cols = slice(c * dc, (c + 1) * dc)
                o_ref[:, cols] += jnp.dot(a_scr[a_prev], v_ref[:, cols], preferred_element_type=F32)
                some_pieces()
        if jb is not None:
            for c in range(tm // lt):
                tok = slice(c * lt, (c + 1) * lt)
                a_scr[slot, tok, :] = (act_scr[:, tok] * g_scr[:, tok]).T.astype(BF16)

    @pl.when(j == 0)
    def _():
        o_ref[...] = jnp.zeros_like(o_ref)
        step(0, 0, None)

    @pl.when(jnp.logical_and(j > 0, j < n_blocks))
    def _():
        slot = lax.rem(j, 2)
        step(j, slot, 1 - slot)

    @pl.when(j == n_blocks)
    def _():
        step(None, None, lax.rem(n_blocks - 1, 2))


def _peer_dense(h2, u, v, thr, e1, s2, e2):
    t, d = h2.shape
    n_exp = u.shape[0]
    n_heads, nk, _ = thr.shape
    tm = _tile(t, 512)
    te = 4 * nk
    n_blocks = n_exp // te
    once = pl.Buffered(1)
    fspec = pl.BlockSpec((n_heads, nk, tm), lambda i, j: (0, 0, i), pipeline_mode=once)
    return pl.pallas_call(
        functools.partial(_peer_dense_kernel, n_heads=n_heads, nk=nk),
        grid=(t // tm, n_blocks + 1),
        in_specs=[pl.BlockSpec((tm, d), lambda i, j: (i, 0), pipeline_mode=once),
                  pl.BlockSpec((te, d), lambda i, j: (jnp.minimum(j, n_blocks - 1), 0)),
                  pl.BlockSpec((te, d), lambda i, j: (jnp.maximum(j - 1, 0), 0)),
                  fspec, fspec, fspec, fspec],
        out_specs=pl.BlockSpec((tm, d), lambda i, j: (i, 0)),
        out_shape=jax.ShapeDtypeStruct((t, d), F32),
        scratch_shapes=[pltpu.VMEM((2, tm, te), BF16), pltpu.VMEM((te, tm), F32), pltpu.VMEM((te, tm), F32)],
        compiler_params=_cp("arbitrary", "arbitrary"),
        name="peer_dense",
    )(h2, u, v, thr, e1, s2, e2)


def _residual_kernel(x_ref, p_ref, gate_ref, g_ref, o_ref, *, final_norm):
    x = x_ref[...] + gate_ref[0] * p_ref[...]
    if final_norm:
        x = x * lax.rsqrt(jnp.mean(x * x, axis=-1, keepdims=True) + EPS) * g_ref[...]
    o_ref[...] = x


def _residual(x1, pe, grp, k_gate, g_final, final_norm):
    t, d = x1.shape
    tm = grp.row_tile(256)
    return pl.pallas_call(
        functools.partial(_residual_kernel, final_norm=final_norm),
        grid=(t // tm, 1),
        in_specs=[pl.BlockSpec((tm, d), lambda i, j: (i, 0)),
                  pl.BlockSpec((tm, d), lambda i, j: (i, 0)),
                  grp.mod_spec(tm, d),
                  pl.BlockSpec((1, d), lambda i, j: (0, 0))],
        out_specs=pl.BlockSpec((tm, d), lambda i, j: (i, 0)),
        out_shape=jax.ShapeDtypeStruct((t, d), F32),
        compiler_params=_cp("arbitrary", "arbitrary"),
        name="residual",
    )(x1, pe, grp.mods[k_gate], g_final.reshape(1, d))


def _softplus(x):
    return jnp.maximum(x, 0.0) + jnp.log1p(jnp.exp(-jnp.abs(x)))


def _layer_group(x, grp, prm, dims, ssd_fn, conv_fn, conf_fn, final_g, final_norm):
    d = dims["d"]
    d_inner = dims["d_inner"]
    off = dims["off"]
    tm = grp.row_tile(1024)
    tn = 1024

    h = _modnorm(x, prm["norm1_g"], grp, 1, 0)
    (a0, na), (b0, nb) = dims["cols_a"], dims["cols_b"]
    tn_a, tn_b = _tile(na, tn), _tile(nb, tn)
    pending = lambda k: isinstance(prm[k], tuple) or prm[k].dtype != BF16

    def mm_rounding(keys, *args, **kwargs):
        nonlocal prm
        jobs = [k for k in keys if pending(k)]
        res = _mm(*args, side=[prm[k] for k in jobs], **kwargs)
        if not jobs:
            return res
        prm = dict(prm, **dict(zip(jobs, res[1:])))
        return res[0]

    proj_a = mm_rounding(("w_b", "w_dt", "w_ssd_out", "w_out"), h, prm["w_a"], tm=tm, tn=tn_a,
                         out_dtype=BF16, name="in_proj_a")
    proj_b = mm_rounding(("peer_u", "w_conf_out", "peer_wq"), h, prm["w_b"], tm=tm, tn=tn_b,
                         out_dtype=BF16, name="in_proj_b")
    hs = prm["w_dt"].shape[1]
    dt = _mm(h, prm["w_dt"], tm=tm, tn=hs, out_dtype=F32,
             epi=lambda acc, bias: _softplus(acc + bias),
             extras=[(prm["dt_bias"].reshape(1, hs), _row_spec(hs))], name="dt_proj")

    xbc = conv_fn(proj_a)
    y, ssm_new = ssd_fn(xbc, dt)
    yn = _gatednorm(y, proj_a, off["z"], prm["ssd_norm_g"], grp.row_tile(256))
    tn_d = _tile(d, 512)
    sa = mm_rounding(("peer_v",), yn, prm["w_ssd_out"], tm=grp.row_tile(512), tn=tn_d, out_dtype=BF16,
                     epi=lambda acc, ga: jax.nn.sigmoid(ga.astype(F32)) * acc,
                     extras=[(proj_b, _tile_spec(grp.row_tile(512), tn_d, off["ga"]))], name="ssd_out")

    vc, conf_state = conf_fn(proj_b)
    tmm = grp.row_tile(1024)
    tn_m = _tile(d, 1024)
    mixed = _mm(vc, prm["w_conf_out"], tm=tmm, tn=tn_m, out_dtype=BF16,
                epi=lambda acc, s, gb: s.astype(F32) + jax.nn.sigmoid(gb.astype(F32)) * acc,
                extras=[(sa, _tile_spec(tmm, tn_m, 0)), (proj_b, _tile_spec(tmm, tn_m, off["gb"]))],
                name="conf_out")
    tn_o = _tile(d, 512)
    x1 = _mm(mixed, prm["w_out"], tm=tmm, tn=tn_o, out_dtype=F32,
             epi=lambda acc, xr, gate: xr + gate * acc,
             extras=[(x, _tile_spec(tmm, tn_o, 0)), (grp.mods[2], grp.mod_spec(tmm, tn_o))],
             name="out_proj")

    h2 = _modnorm(x1, prm["norm2_g"], grp, 4, 3)
    qv = _mm(h2, prm["peer_wq"], tm=tmm, tn=_tile(prm["peer_wq"].shape[1], 1024), out_dtype=F32, name="peer_q")
    thr, e1, s2, e2 = _peer_topk(qv, prm["peer_keys1"], prm["peer_keys2"])
    pe = _peer_dense(h2, prm["peer_u"], prm["peer_v"], thr, e1, s2, e2)
    x2 = _residual(x1, pe, grp, 5, final_g, final_norm)
    return x2, proj_a, ssm_new, conf_state, prm


def kernel(x_prompt, x_sample, c_prompt, c_sample, state_ssm, state_ssd_conv, state_conf_conv, w_ada, b_ada, norm1_g, w_in, ssd_conv_w, ssd_conv_b, dt_bias, a_log, d_skip, ssd_norm_g, w_ssd_out, conf_dw_w, conf_dw_b, conf_ln_g, conf_ln_b, w_conf_out, w_out, norm2_g, peer_wq, peer_keys1, peer_keys2, peer_u, peer_v, final_norm_g):
    depth = w_ada.shape[0]
    bp, lp, d = x_prompt.shape
    bs, ls, _ = x_sample.shape
    n_state = state_ssm.shape[-1]
    hdim = state_ssm.shape[-2]
    n_ssm_heads = a_log.shape[-1]
    d_inner = ssd_norm_g.shape[-1]
    conv_dim = ssd_conv_w.shape[-1]
    groups = (conv_dim - d_inner) // (2 * n_state)
    heads = n_ssm_heads // groups
    d_conf = conf_dw_w.shape[-1]
    kc = conf_dw_w.shape[-2]
    ks = ssd_conv_w.shape[-2]
    n_mod = w_ada.shape[-1] // d

    c_xbc, c_dt = d_inner + conv_dim, d_inner + conv_dim + n_ssm_heads
    off = {"z": 0, "xbc": d_inner,
           "ca": 0, "cg": d_conf, "ga": 2 * d_conf, "gb": 2 * d_conf + d}
    ncols_a, ncols_b = c_xbc, 2 * d_conf + 2 * d
    dims = {"d": d, "d_inner": d_inner, "off": off, "cols_a": (0, ncols_a), "cols_b": (c_dt, ncols_b)}

    xp = x_prompt.reshape(bp * lp, d)
    xs = x_sample.transpose(1, 0, 2).reshape(ls * bs, d)
    rows_c = bp + bs
    rows_pad = -(-rows_c // SUBLANES) * SUBLANES
    c_all = jnp.pad(jnp.concatenate([c_prompt, c_sample], axis=0), ((0, rows_pad - rows_c), (0, 0)))

    outs = {k: [] for k in ("ssm_p", "sconv_p", "cconv_p", "ssm_s", "sconv_s", "cconv_s")}
    for li in range(depth):
        final_norm = li == depth - 1
        mod = _adaln(c_all, w_ada[li], b_ada[li])
        mod_p = mod[:bp].reshape(bp, n_mod, d)
        mod_s = mod[bp:bp + bs].reshape(bs, n_mod, d)
        grp_p = _Group(bp * lp, lp, False, [mod_p[:, k][:, None, :] for k in range(n_mod)])
        grp_s = _Group(ls * bs, ls, True, [jnp.tile(mod_s[:, k], (ls, 1))[None] for k in range(n_mod)])

        wi = w_in[li]
        prm = {
            "norm1_g": norm1_g[li], "norm2_g": norm2_g[li], "ssd_norm_g": ssd_norm_g[li],
            "w_a": wi[:, :c_xbc].astype(BF16), "w_b": (wi, c_dt, ncols_b), "w_dt": (wi, c_xbc, c_dt - c_xbc),
            "dt_bias": dt_bias[li].astype(F32),
            "peer_keys1": peer_keys1[li], "peer_keys2": peer_keys2[li],
            "w_ssd_out": w_ssd_out[li], "w_conf_out": w_conf_out[li], "w_out": w_out[li],
            "peer_wq": peer_wq[li], "peer_u": peer_u[li], "peer_v": peer_v[li],
        }

        def conv_p(proj):
            hb = jnp.zeros((bp, SUBLANES, conv_dim), F32)
            return _ssdconv_prompt(proj.reshape(bp, lp, ncols_a), hb, ssd_conv_w[li], ssd_conv_b[li],
                                   off["xbc"], BF16).reshape(bp * lp, conv_dim)

        def ssd_p(xbc, dt):
            s0 = jnp.zeros((bp, n_ssm_heads * hdim, n_state), F32)
            return _ssd_prompt(xbc, dt, a_log[li], d_skip[li], s0, bp, lp, groups, heads, hdim, n_state)

        def conf_p(proj):
            hb = jnp.zeros((bp, CONF_HALO, d_conf), F32)
            return _conf_prompt(proj.reshape(bp, lp, ncols_b), hb, conf_dw_w[li], conf_dw_b[li],
                                conf_ln_g[li], conf_ln_b[li], off["ca"], off["cg"])

        xp, proj_p, ssm_p, tail_p, prm = _layer_group(xp, grp_p, prm, dims, ssd_p, conv_p, conf_p,
                                                 final_norm_g, final_norm)
        outs["ssm_p"].append(ssm_p.reshape(bp, n_ssm_heads, hdim, n_state))
        keep_p = min(lp, ks - 1)
        xbc_tail_p = proj_p.reshape(bp, lp, ncols_a)[:, lp - keep_p:, off["xbc"]:].astype(F32)
        sconv0 = jnp.zeros((bp, ks - 1 - keep_p, conv_dim), F32)
        outs["sconv_p"].append(jnp.concatenate([sconv0, xbc_tail_p], axis=1))
        outs["cconv_p"].append(tail_p[:, CONF_HALO - (kc - 1):])

        sbuf = state_ssd_conv[li]
        cbuf = state_conf_conv[li]
        ssm0 = state_ssm[li].reshape(bs, n_ssm_heads * hdim, n_state)

        def conv_s(proj):
            return _ssdconv_sample(proj.reshape(ls, bs, ncols_a), sbuf.transpose(1, 0, 2), ssd_conv_w[li],
                                   ssd_conv_b[li], off["xbc"], BF16).reshape(ls * bs, conv_dim)

        def ssd_s(xbc, dt):
            xbc3 = xbc.reshape(ls, bs, conv_dim)
            yp, ea, xw, cd = _ssd_sample_a(xbc3, dt, a_log[li], d_skip[li], groups, heads, hdim, n_state)
            gn = groups * n_state

            def bmajor(v, rows):
                return jnp.pad(v.transpose(1, 0, 2), ((0, 0), (0, rows - ls), (0, 0)))

            r16 = -(-ls // 16) * 16
            r8 = -(-ls // SUBLANES) * SUBLANES
            b_b = bmajor(xbc3[:, :, d_inner:d_inner + gn], r16)
            c_b = bmajor(xbc3[:, :, d_inner + gn:], r16)
            cd_col = cd.transpose(1, 0, 2).reshape(bs, groups * heads, 1)
            y_b, s_new = _ssd_sample_b(ssm0, c_b, b_b, bmajor(xw, r16), bmajor(ea, r8), bmajor(yp, r8),
                                       cd_col, groups, heads, hdim)
            y = y_b[:, :ls].transpose(1, 0, 2).reshape(ls * bs, d_inner).astype(BF16)
            return y, s_new

        def conf_s(proj):
            vc, cnew = _conf_sample(proj.reshape(ls, bs, ncols_b), cbuf, conf_dw_w[li],
                                    conf_dw_b[li], conf_ln_g[li], conf_ln_b[li], off["ca"], off["cg"])
            return vc.reshape(ls * bs, d_conf), cnew

        xs, proj_s, ssm_s, cconv_s, _ = _layer_group(xs, grp_s, prm, dims, ssd_s, conv_s, conf_s,
                                                  final_norm_g, final_norm)
        outs["ssm_s"].append(ssm_s.reshape(bs, n_ssm_heads, hdim, n_state))
        keep_s = min(ls, ks - 1)
        xbc_tail_s = proj_s.reshape(ls, bs, ncols_a)[ls - keep_s:, :, off["xbc"]:].astype(F32).transpose(1, 0, 2)
        outs["sconv_s"].append(jnp.concatenate([sbuf[:, keep_s:], xbc_tail_s], axis=1))
        outs["cconv_s"].append(cconv_s)

    y_prompt = xp.reshape(bp, lp, d)
    y_sample = xs.reshape(ls, bs, d).transpose(1, 0, 2)
    return (y_prompt, y_sample,
            jnp.stack(outs["ssm_p"], 0), jnp.stack(outs["sconv_p"], 0), jnp.stack(outs["cconv_p"], 0),
            jnp.stack(outs["ssm_s"], 0), jnp.stack(outs["sconv_s"], 0), jnp.stack(outs["cconv_s"], 0))
```
